```python
import jax, jax.numpy as jnp
from jax import lax
import numpy as np

D_MODEL = 1024
BATCH = 2
SEQ = 8192
DEPTH = 2

D_MIX = D_MODEL
POOL_WIDTH = D_MIX // 2
POOL_WINDOWS = (2, 4, 8, 16)
POOL_GROUPS = len(POOL_WINDOWS)
POOL_GROUP_DIM = POOL_WIDTH // POOL_GROUPS
SGU_WIDTH = D_MIX - POOL_WIDTH
SGU_HEADS = 4
SGU_HEAD_DIM = SGU_WIDTH // SGU_HEADS
CHUNK = 128
EVEN_IN = POOL_WIDTH + 2 * SGU_WIDTH
CONV_WIDTH = 3
CONV_DIM = D_MIX
D_FF = 4 * D_MODEL
PLE_DIM = 256
LN_EPS = 1e-5
N_EVEN = (DEPTH + 1) // 2
N_ODD = DEPTH // 2
DEEPNORM_ALPHA = (2.0 * DEPTH) ** 0.25
DEEPNORM_BETA = (8.0 * DEPTH) ** -0.25

kernel_name = "hybrid_pool_sgu_shortconv_deepnorm_trunk"


def layer_norm(x, g, b):
    xf = x.astype(jnp.float32)
    mu = jnp.mean(xf, axis=-1, keepdims=True)
    var = jnp.mean(jnp.square(xf - mu), axis=-1, keepdims=True)
    y = (xf - mu) * lax.rsqrt(var + LN_EPS)
    return (y * g.astype(jnp.float32) + b.astype(jnp.float32)).astype(x.dtype)


def pool_mixer(a, pool_w, pool_scale):
    bsz, s, _ = a.shape
    ag = a.reshape(bsz, s, POOL_GROUPS, POOL_GROUP_DIM)
    t = jnp.arange(s)
    outs = []
    for g, w in enumerate(POOL_WINDOWS):
        xg = ag[:, :, g].astype(jnp.float32)
        cs = jnp.cumsum(xg, axis=1)
        lag = jnp.pad(cs, ((0, 0), (w, 0), (0, 0)))[:, :s]
        cnt = jnp.minimum(t + 1, w).astype(jnp.float32)[None, :, None]
        pooled = (cs - lag) / cnt - xg
        outs.append(jnp.einsum('bsc,cd->bsd', pooled.astype(a.dtype), pool_w[g]))
    return jnp.concatenate(outs, axis=-1) * pool_scale


def spatial_gating(u, v, ln_g, ln_b, w_s, b_s):
    bsz, s, _ = u.shape
    vn = layer_norm(v, ln_g, ln_b)
    vc = vn.reshape(bsz, s // CHUNK, CHUNK, SGU_HEADS, SGU_HEAD_DIM)
    mask = jnp.tril(jnp.ones((CHUNK, CHUNK), w_s.dtype))
    mixed = jnp.einsum('hts,bnshc->bnthc', w_s * mask, vc) + b_s.T[None, None, :, :, None]
    return u * mixed.reshape(bsz, s, SGU_WIDTH)


def short_conv_mixer(x, w_in, conv_w, w_out):
    proj = jnp.einsum('bsd,de->bse', x, w_in)
    gate_b, gate_c, h = jnp.split(proj, 3, axis=-1)
    z = gate_c * h
    zp = jnp.pad(z, ((0, 0), (CONV_WIDTH - 1, 0), (0, 0)))
    s = x.shape[1]
    conv = sum(conv_w[k] * zp[:, k:k + s] for k in range(CONV_WIDTH))
    return jnp.einsum('bse,ed->bsd', gate_b * conv, w_out)


def setup_inputs(seed: int = 0) -> dict:
    key = jax.random.key(seed)
    ks = jax.random.split(key, 24)
    f32 = jnp.float32
    nrm = lambda k, shape, scale: jax.random.normal(k, shape, f32) * scale
    return {
        "x": nrm(ks[0], (BATCH, SEQ, D_MODEL), 1.0),
        "p": nrm(ks[1], (DEPTH, BATCH, SEQ, PLE_DIM), 1.0),
        "ev_w_in": nrm(ks[2], (N_EVEN, D_MODEL, EVEN_IN), D_MODEL ** -0.5),
        "ev_pool_w": nrm(ks[3], (N_EVEN, POOL_GROUPS, POOL_GROUP_DIM, POOL_GROUP_DIM), POOL_GROUP_DIM ** -0.5),
        "ev_pool_scale": 1.0 + nrm(ks[4], (N_EVEN, POOL_WIDTH), 0.05),
        "ev_sgu_ln_g": 1.0 + nrm(ks[5], (N_EVEN, SGU_WIDTH), 0.05),
        "ev_sgu_ln_b": nrm(ks[6], (N_EVEN, SGU_WIDTH), 0.02),
        "ev_sgu_w": nrm(ks[7], (N_EVEN, SGU_HEADS, CHUNK, CHUNK), CHUNK ** -0.5),
        "ev_sgu_b": 1.0 + nrm(ks[8], (N_EVEN, SGU_HEADS, CHUNK), 0.01),
        "ev_w_out": nrm(ks[9], (N_EVEN, D_MIX, D_MODEL), D_MIX ** -0.5 * DEEPNORM_BETA),
        "od_w_in": nrm(ks[10], (N_ODD, D_MODEL, 3 * CONV_DIM), D_MODEL ** -0.5),
        "od_conv_w": nrm(ks[11], (N_ODD, CONV_WIDTH, CONV_DIM), CONV_WIDTH ** -0.5),
        "od_w_out": nrm(ks[12], (N_ODD, CONV_DIM, D_MODEL), CONV_DIM ** -0.5 * DEEPNORM_BETA),
        "ln1_g": 1.0 + nrm(ks[13], (DEPTH, D_MODEL), 0.05),
        "ln1_b": nrm(ks[14], (DEPTH, D_MODEL), 0.02),
        "ffn_w1": nrm(ks[15], (DEPTH, D_MODEL, D_FF), D_MODEL ** -0.5),
        "ffn_w2": nrm(ks[16], (DEPTH, D_FF, D_MODEL), D_FF ** -0.5 * DEEPNORM_BETA),
        "ln2_g": 1.0 + nrm(ks[17], (DEPTH, D_MODEL), 0.05),
        "ln2_b": nrm(ks[18], (DEPTH, D_MODEL), 0.02),
        "ple_gate_w": nrm(ks[19], (DEPTH, D_MODEL, D_MODEL), D_MODEL ** -0.5),
        "ple_w": nrm(ks[20], (DEPTH, PLE_DIM, D_MODEL), PLE_DIM ** -0.5),
    }


def reference(x, p, ev_w_in, ev_pool_w, ev_pool_scale, ev_sgu_ln_g, ev_sgu_ln_b,
              ev_sgu_w, ev_sgu_b, ev_w_out, od_w_in, od_conv_w, od_w_out,
              ln1_g, ln1_b, ffn_w1, ffn_w2, ln2_g, ln2_b, ple_gate_w, ple_w):
    for i in range(DEPTH):
        j = i // 2
        if i % 2 == 0:
            proj = jnp.einsum('bsd,de->bse', x, ev_w_in[j])
            a = proj[..., :POOL_WIDTH]
            u = proj[..., POOL_WIDTH:POOL_WIDTH + SGU_WIDTH]
            v = proj[..., POOL_WIDTH + SGU_WIDTH:]
            y_a = pool_mixer(a, ev_pool_w[j], ev_pool_scale[j])
            y_b = spatial_gating(u, v, ev_sgu_ln_g[j], ev_sgu_ln_b[j], ev_sgu_w[j], ev_sgu_b[j])
            mix = jnp.einsum('bse,ed->bsd', jnp.concatenate([y_a, y_b], axis=-1), ev_w_out[j])
        else:
            mix = short_conv_mixer(x, od_w_in[j], od_conv_w[j], od_w_out[j])
        x = layer_norm(DEEPNORM_ALPHA * x + mix, ln1_g[i], ln1_b[i])
        hid = jnp.square(jax.nn.relu(jnp.einsum('bsd,df->bsf', x, ffn_w1[i])))
        x = layer_norm(DEEPNORM_ALPHA * x + jnp.einsum('bsf,fd->bsd', hid, ffn_w2[i]), ln2_g[i], ln2_b[i])
        gate = jax.nn.sigmoid(jnp.einsum('bsd,de->bse', x, ple_gate_w[i]))
        x = x + gate * jnp.einsum('bsk,kd->bsd', p[i], ple_w[i])
    return x
```

```python
import functools

import jax
import jax.numpy as jnp
from jax import lax
from jax.experimental import pallas as pl
from jax.experimental.pallas import tpu as pltpu

POOL_WINDOWS = (2, 4, 8, 16)
POOL_GROUP_DIM = 128
POOL_WIDTH = POOL_GROUP_DIM * len(POOL_WINDOWS)
SGU_HEADS = 4
SGU_HEAD_DIM = 128
SGU_WIDTH = SGU_HEADS * SGU_HEAD_DIM
CHUNK = 128
CONV_WIDTH = 3
LN_EPS = 1e-5
DEPTH = 2
DEEPNORM_ALPHA = (2.0 * DEPTH) ** 0.25

SUBLANES = 8
POOL_HALO = max(POOL_WINDOWS)
POOL_PAD = SUBLANES
CONV_HALO = SUBLANES

SEQ_TILE = 512
VMEM_LIMIT_BYTES = 56 * 1024 * 1024

_BF16 = jnp.bfloat16
_F32 = jnp.float32


def _dot(a, b):
    return jnp.dot(a, b, preferred_element_type=_F32)


def _layer_norm(x, g, b):
    mu = jnp.mean(x, axis=-1, keepdims=True)
    xc = x - mu
    var = jnp.mean(xc * xc, axis=-1, keepdims=True)
    return xc * lax.rsqrt(var + LN_EPS) * g + b


def _even_mixer_kernel(x_ref, w_in_ref, pool_bd_ref, pool_scale_ref, sgu_g_ref, sgu_b_ref,
                       sgu_w_ref, sgu_bias_ref, w_out_ref, ln1_g_ref, ln1_b_ref,
                       o_ref, ext_ref, lvl_a_ref, lvl_b_ref, y_ref):
    s = pl.program_id(1)
    ts = x_ref.shape[0]
    base = POOL_PAD + POOL_HALO
    rows = ts + POOL_HALO

    xf = x_ref[...]
    xb = xf.astype(_BF16)

    @pl.when(s == 0)
    def _():
        ext_ref[0:base, :] = jnp.zeros((base, POOL_WIDTH), _F32)
        lvl_a_ref[0:POOL_PAD, :] = jnp.zeros((POOL_PAD, POOL_WIDTH), _F32)
        lvl_b_ref[0:POOL_PAD, :] = jnp.zeros((POOL_PAD, POOL_WIDTH), _F32)

    a = _dot(xb, w_in_ref[:, 0:POOL_WIDTH])
    ext_ref[base:base + ts, :] = a
    g = POOL_GROUP_DIM
    lo = POOL_PAD
    lvl_a_ref[lo:lo + rows, :] = ext_ref[lo:lo + rows, :] + ext_ref[lo - 1:lo - 1 + rows, :]
    lvl_b_ref[lo:lo + rows, g:] = lvl_a_ref[lo:lo + rows, g:] + lvl_a_ref[lo - 2:lo - 2 + rows, g:]
    lvl_a_ref[lo:lo + rows, 2 * g:] = (lvl_b_ref[lo:lo + rows, 2 * g:]
                                       + lvl_b_ref[lo - 4:lo - 4 + rows, 2 * g:])
    sums = (
        lvl_a_ref[base:base + ts, 0:g],
        lvl_b_ref[base:base + ts, g:2 * g],
        lvl_a_ref[base:base + ts, 2 * g:3 * g],
        lvl_a_ref[base:base + ts, 3 * g:4 * g] + lvl_a_ref[base - 8:base - 8 + ts, 3 * g:4 * g],
    )
    t1 = lax.broadcasted_iota(jnp.int32, (ts, g), 0) + (s * ts + 1)
    pooled = []
    for gi, w in enumerate(POOL_WINDOWS):
        cnt = jnp.minimum(t1, w).astype(_F32)
        pooled.append((sums[gi] / cnt - a[:, gi * g:(gi + 1) * g]).astype(_BF16))
    ext_ref[POOL_PAD:base, :] = a[ts - POOL_HALO:ts, :]

    scale = pool_scale_ref[...]
    for pair in range(2):
        pb = jnp.concatenate(pooled[2 * pair:2 * pair + 2], axis=1)
        ya = _dot(pb, pool_bd_ref[pair]) * scale[:, 2 * g * pair:2 * g * (pair + 1)]
        y_ref[:, 2 * g * pair:2 * g * (pair + 1)] = ya.astype(_BF16)

    u = _dot(xb, w_in_ref[:, POOL_WIDTH:POOL_WIDTH + SGU_WIDTH])
    v = _dot(xb, w_in_ref[:, POOL_WIDTH + SGU_WIDTH:])
    vn = _layer_norm(v, sgu_g_ref[...], sgu_b_ref[...]).astype(_BF16)
    n_chunks = ts // CHUNK
    tri = (lax.broadcasted_iota(jnp.int32, (CHUNK, CHUNK), 0)
           >= lax.broadcasted_iota(jnp.int32, (CHUNK, CHUNK), 1))
    d = SGU_HEAD_DIM
    for h in range(SGU_HEADS):
        w_h = jnp.where(tri, sgu_w_ref[h], 0.0).astype(_BF16)
        rhs = jnp.concatenate(
            [vn[c * CHUNK:(c + 1) * CHUNK, h * d:(h + 1) * d] for c in range(n_chunks)], axis=1)
        mixed = _dot(w_h, rhs)
        bias_h = sgu_bias_ref[:, h * d:(h + 1) * d]
        for c in range(n_chunks):
            m = mixed[:, c * d:(c + 1) * d] + bias_h
            yb = u[c * CHUNK:(c + 1) * CHUNK, h * d:(h + 1) * d] * m
            y_ref[c * CHUNK:(c + 1) * CHUNK,
                  POOL_WIDTH + h * d:POOL_WIDTH + (h + 1) * d] = yb.astype(_BF16)

    mix = _dot(y_ref[...], w_out_ref[...])
    o_ref[...] = _layer_norm(DEEPNORM_ALPHA * xf + mix, ln1_g_ref[...], ln1_b_ref[...])


def _odd_mixer_kernel(x_ref, w_in_ref, conv_w_ref, w_out_ref, ln1_g_ref, ln1_b_ref,
                      o_ref, z_ref):
    s = pl.program_id(1)
    ts = x_ref.shape[0]
    cd = w_out_ref.shape[0]
    xf = x_ref[...]
    xb = xf.astype(_BF16)

    @pl.when(s == 0)
    def _():
        z_ref[0:CONV_HALO, :] = jnp.zeros((CONV_HALO, cd), _F32)

    gate_c = _dot(xb, w_in_ref[:, cd:2 * cd])
    hid = _dot(xb, w_in_ref[:, 2 * cd:3 * cd])
    z = gate_c * hid
    z_ref[CONV_HALO:CONV_HALO + ts, :] = z
    cw = conv_w_ref[...]
    conv = cw[CONV_WIDTH - 1:CONV_WIDTH, :] * z
    for k in range(CONV_WIDTH - 1):
        off = CONV_HALO - (CONV_WIDTH - 1) + k
        conv = conv + cw[k:k + 1, :] * z_ref[off:off + ts, :]
    z_ref[0:CONV_HALO, :] = z[ts - CONV_HALO:ts, :]

    gate_b = _dot(xb, w_in_ref[:, 0:cd])
    y = (gate_b * conv).astype(_BF16)
    mix = _dot(y, w_out_ref[...])
    o_ref[...] = _layer_norm(DEEPNORM_ALPHA * xf + mix, ln1_g_ref[...], ln1_b_ref[...])


def _ffn_kernel(x_ref, p_ref, w1_ref, w2_ref, ln2_g_ref, ln2_b_ref, wg_ref, wp_ref, o_ref,
                *, ff_chunk):
    xf = x_ref[...]
    xb = xf.astype(_BF16)
    d_ff = w1_ref.shape[1]
    acc = None
    for c in range(d_ff // ff_chunk):
        h = _dot(xb, w1_ref[:, c * ff_chunk:(c + 1) * ff_chunk])
        h = jnp.maximum(h, 0.0)
        hb = (h * h).astype(_BF16)
        part = _dot(hb, w2_ref[c * ff_chunk:(c + 1) * ff_chunk, :])
        acc = part if acc is None else acc + part
    x2 = _layer_norm(DEEPNORM_ALPHA * xf + acc, ln2_g_ref[...], ln2_b_ref[...])
    gate = jax.nn.sigmoid(_dot(x2.astype(_BF16), wg_ref[...]))
    pp = _dot(p_ref[...].astype(_BF16), wp_ref[...])
    o_ref[...] = x2 + gate * pp


def _resident(shape):
    zeros = (0,) * len(shape)
    return pl.BlockSpec(shape, lambda b, s: zeros, pipeline_mode=pl.Buffered(1))


def _compiler_params():
    return pltpu.CompilerParams(dimension_semantics=("arbitrary", "arbitrary"),
                                vmem_limit_bytes=VMEM_LIMIT_BYTES)


def _row(v):
    return v.reshape(1, -1)


def _even_mixer(x, w_in, pool_bd, pool_scale, sgu_g, sgu_b, sgu_w, sgu_bias, w_out, g1, b1):
    bsz, seq, d = x.shape
    ts = SEQ_TILE
    act = pl.BlockSpec((None, ts, d), lambda b, s: (b, s, 0))
    args = (w_in, pool_bd, pool_scale, sgu_g, sgu_b, sgu_w, sgu_bias, w_out, g1, b1)
    buf_rows = POOL_PAD + POOL_HALO + ts
    return pl.pallas_call(
        _even_mixer_kernel,
        grid=(bsz, seq // ts),
        in_specs=[act] + [_resident(a.shape) for a in args],
        out_specs=act,
        out_shape=jax.ShapeDtypeStruct(x.shape, x.dtype),
        scratch_shapes=[pltpu.VMEM((buf_rows, POOL_WIDTH), _F32),
                        pltpu.VMEM((buf_rows, POOL_WIDTH), _F32),
                        pltpu.VMEM((buf_rows, POOL_WIDTH), _F32),
                        pltpu.VMEM((ts, d), _BF16)],
        compiler_params=_compiler_params(),
        name="even_mixer",
    )(x, *args)


def _odd_mixer(x, w_in, conv_w, w_out, g1, b1):
    bsz, seq, d = x.shape
    ts = SEQ_TILE
    act = pl.BlockSpec((None, ts, d), lambda b, s: (b, s, 0))
    args = (w_in, conv_w, w_out, g1, b1)
    return pl.pallas_call(
        _odd_mixer_kernel,
        grid=(bsz, seq // ts),
        in_specs=[act] + [_resident(a.shape) for a in args],
        out_specs=act,
        out_shape=jax.ShapeDtypeStruct(x.shape, x.dtype),
        scratch_shapes=[pltpu.VMEM((CONV_HALO + ts, w_out.shape[0]), _F32)],
        compiler_params=_compiler_params(),
        name="odd_mixer",
    )(x, *args)


def _ffn(x, p, layer, w1, w2, g2, b2, wg, wp):
    bsz, seq, d = x.shape
    ts = SEQ_TILE
    act = pl.BlockSpec((None, ts, d), lambda b, s: (b, s, 0))
    p_spec = pl.BlockSpec((None, None, ts, p.shape[-1]), lambda b, s: (layer, b, s, 0))
    args = (w1, w2, g2, b2, wg, wp)
    return pl.pallas_call(
        functools.partial(_ffn_kernel, ff_chunk=d),
        grid=(bsz, seq // ts),
        in_specs=[act, p_spec] + [_resident(a.shape) for a in args],
        out_specs=act,
        out_shape=jax.ShapeDtypeStruct(x.shape, x.dtype),
        compiler_params=_compiler_params(),
        name=f"ffn_{layer}",
    )(x, p, *args)


def _block_diag_pairs(pool_w):
    g = POOL_GROUP_DIM
    z = jnp.zeros((g, g), pool_w.dtype)
    pairs = []
    for k in range(0, len(POOL_WINDOWS), 2):
        top = jnp.concatenate([pool_w[k], z], axis=1)
        bot = jnp.concatenate([z, pool_w[k + 1]], axis=1)
        pairs.append(jnp.concatenate([top, bot], axis=0))
    return jnp.stack(pairs)


def kernel(x, p, ev_w_in, ev_pool_w, ev_pool_scale, ev_sgu_ln_g, ev_sgu_ln_b, ev_sgu_w, ev_sgu_b,
           ev_w_out, od_w_in, od_conv_w, od_w_out, ln1_g, ln1_b, ffn_w1, ffn_w2, ln2_g, ln2_b,
           ple_gate_w, ple_w):
    assert x.shape[1] % SEQ_TILE == 0 and SEQ_TILE % CHUNK == 0
    bf = lambda w: w.astype(_BF16)
    for i in range(DEPTH):
        j = i // 2
        if i % 2 == 0:
            sgu_bias = jnp.repeat(ev_sgu_b[j].T, SGU_HEAD_DIM, axis=1)
            x = _even_mixer(x, bf(ev_w_in[j]), bf(_block_diag_pairs(ev_pool_w[j])),
                            _row(ev_pool_scale[j]), _row(ev_sgu_ln_g[j]), _row(ev_sgu_ln_b[j]),
                            ev_sgu_w[j], sgu_bias, bf(ev_w_out[j]),
                            _row(ln1_g[i]), _row(ln1_b[i]))
        else:
            x = _odd_mixer(x, bf(od_w_in[j]), od_conv_w[j], bf(od_w_out[j]),
                           _row(ln1_g[i]), _row(ln1_b[i]))
        x = _ffn(x, p, i, bf(ffn_w1[i]), bf(ffn_w2[i]), _row(ln2_g[i]), _row(ln2_b[i]),
                 bf(ple_gate_w[i]), bf(ple_w[i]))
    return x
```

```python
import functools

import jax
import jax.numpy as jnp
from jax import lax
from jax.experimental import pallas as pl
from jax.experimental.pallas import tpu as pltpu

POOL_WINDOWS = (2, 4, 8, 16)
POOL_GROUP_DIM = 128
POOL_WIDTH = POOL_GROUP_DIM * len(POOL_WINDOWS)
SGU_HEADS = 4
SGU_HEAD_DIM = 128
SGU_WIDTH = SGU_HEADS * SGU_HEAD_DIM
CHUNK = 128
CONV_WIDTH = 3
LN_EPS = 1e-5
DEPTH = 2
DEEPNORM_ALPHA = (2.0 * DEPTH) ** 0.25

SUBLANES = 8
POOL_HALO = max(POOL_WINDOWS)
POOL_PAD = SUBLANES
POOL_BASE = POOL_PAD + POOL_HALO
CONV_HALO = SUBLANES

SEQ_TILE = 512
SUB_ROWS = 256
VMEM_LIMIT_BYTES = 56 * 1024 * 1024

_BF16 = jnp.bfloat16
_F32 = jnp.float32


def _dot(a, b):
    return jnp.dot(a, b, preferred_element_type=_F32)


def _layer_norm(x, g, b):
    mu = jnp.mean(x, axis=-1, keepdims=True)
    xc = x - mu
    var = jnp.mean(xc * xc, axis=-1, keepdims=True)
    return xc * lax.rsqrt(var + LN_EPS) * g + b


def _chains(x_ref):
    bsz, ts, _ = x_ref.shape
    return [(b, r) for b in range(bsz) for r in range(0, ts, SUB_ROWS)]


def _even_mixer_kernel(x_ref, w_in_ref, pool_bd_ref, pool_scale_ref, sgu_g_ref, sgu_b_ref,
                       sgu_w_ref, sgu_bias_ref, w_out_ref, ln1_g_ref, ln1_b_ref,
                       o_ref, ext_ref, lvl2_ref, lvl4_ref, lvl8_ref, y_ref):
    s = pl.program_id(0)
    bsz, ts, _ = x_ref.shape
    sub, g, d, base = SUB_ROWS, POOL_GROUP_DIM, SGU_HEAD_DIM, POOL_BASE
    chains = _chains(x_ref)

    @pl.when(s == 0)
    def _():
        for b in range(bsz):
            ext_ref[b, 0:base, :] = jnp.zeros((base, POOL_WIDTH), _F32)
            lvl2_ref[b, 0:POOL_PAD, :] = jnp.zeros((POOL_PAD, POOL_WIDTH), _F32)
            lvl4_ref[b, 0:POOL_PAD, :] = jnp.zeros((POOL_PAD, POOL_WIDTH), _F32)

    xf, a, u, v, pooled, vn, mix = {}, {}, {}, {}, {}, {}, {}
    for ch in chains:
        b, r = ch
        xf[ch] = x_ref[b, r:r + sub, :]
        xb = xf[ch].astype(_BF16)
        a[ch] = _dot(xb, w_in_ref[:, 0:POOL_WIDTH])
        ext_ref[b, base + r:base + r + sub, :] = a[ch]
        u[ch] = _dot(xb, w_in_ref[:, POOL_WIDTH:POOL_WIDTH + SGU_WIDTH])
        v[ch] = _dot(xb, w_in_ref[:, POOL_WIDTH + SGU_WIDTH:])

    for ch in chains:
        b, r = ch
        lo = POOL_PAD if r == 0 else base + r
        hi = base + r + sub
        own = slice(base + r, hi)
        lvl2_ref[b, lo:hi, :] = ext_ref[b, lo:hi, :] + ext_ref[b, lo - 1:hi - 1, :]
        lvl4_ref[b, lo:hi, g:] = lvl2_ref[b, lo:hi, g:] + lvl2_ref[b, lo - 2:hi - 2, g:]
        lvl8_ref[b, lo:hi, 2 * g:] = lvl4_ref[b, lo:hi, 2 * g:] + lvl4_ref[b, lo - 4:hi - 4, 2 * g:]
        sums = (
            lvl2_ref[b, own, 0:g],
            lvl4_ref[b, own, g:2 * g],
            lvl8_ref[b, own, 2 * g:3 * g],
            lvl8_ref[b, own, 3 * g:4 * g] + lvl8_ref[b, base + r - 8:hi - 8, 3 * g:4 * g],
        )
        t1 = lax.broadcasted_iota(jnp.int32, (sub, g), 0) + (s * ts + r + 1)
        parts = []
        for gi, w in enumerate(POOL_WINDOWS):
            cnt = jnp.minimum(t1, w).astype(_F32)
            parts.append((sums[gi] / cnt - a[ch][:, gi * g:(gi + 1) * g]).astype(_BF16))
        pooled[ch] = parts
        vn[ch] = _layer_norm(v[ch], sgu_g_ref[...], sgu_b_ref[...]).astype(_BF16)

    for b in range(bsz):
        ext_ref[b, POOL_PAD:base, :] = a[(b, ts - sub)][sub - POOL_HALO:sub, :]

    scale = pool_scale_ref[...]
    n_chunks = sub // CHUNK
    tri = (lax.broadcasted_iota(jnp.int32, (CHUNK, CHUNK), 0)
           >= lax.broadcasted_iota(jnp.int32, (CHUNK, CHUNK), 1))
    w_tri = [jnp.where(tri, sgu_w_ref[h], 0.0).astype(_BF16) for h in range(SGU_HEADS)]
    for ch in chains:
        b, r = ch
        for pair in range(2):
            pb = jnp.concatenate(pooled[ch][2 * pair:2 * pair + 2], axis=1)
            ya = _dot(pb, pool_bd_ref[pair]) * scale[:, 2 * g * pair:2 * g * (pair + 1)]
            y_ref[b, r:r + sub, 2 * g * pair:2 * g * (pair + 1)] = ya.astype(_BF16)
        for h in range(SGU_HEADS):
            rhs = jnp.concatenate(
                [vn[ch][c * CHUNK:(c + 1) * CHUNK, h * d:(h + 1) * d] for c in range(n_chunks)],
                axis=1)
            mixed = _dot(w_tri[h], rhs)
            bias_h = sgu_bias_ref[:, h * d:(h + 1) * d]
            for c in range(n_chunks):
                m = mixed[:, c * d:(c + 1) * d] + bias_h
                yb = u[ch][c * CHUNK:(c + 1) * CHUNK, h * d:(h + 1) * d] * m
                y_ref[b, r + c * CHUNK:r + (c + 1) * CHUNK,
                      POOL_WIDTH + h * d:POOL_WIDTH + (h + 1) * d] = yb.astype(_BF16)

    for ch in chains:
        b, r = ch
        mix[ch] = _dot(y_ref[b, r:r + sub, :], w_out_ref[...])
    for ch in chains:
        b, r = ch
        o_ref[b, r:r + sub, :] = _layer_norm(DEEPNORM_ALPHA * xf[ch] + mix[ch],
                                             ln1_g_ref[...], ln1_b_ref[...])


def _odd_mixer_kernel(x_ref, w_in_ref, conv_w_ref, w_out_ref, ln1_g_ref, ln1_b_ref,
                      o_ref, z_ref):
    s = pl.program_id(0)
    bsz, ts, _ = x_ref.shape
    sub = SUB_ROWS
    cd = w_out_ref.shape[0]
    chains = _chains(x_ref)

    @pl.when(s == 0)
    def _():
        for b in range(bsz):
            z_ref[b, 0:CONV_HALO, :] = jnp.zeros((CONV_HALO, cd), _F32)

    xf, z, gate_b, y, mix = {}, {}, {}, {}, {}
    for ch in chains:
        b, r = ch
        xf[ch] = x_ref[b, r:r + sub, :]
        xb = xf[ch].astype(_BF16)
        gate_c = _dot(xb, w_in_ref[:, cd:2 * cd])
        hid = _dot(xb, w_in_ref[:, 2 * cd:3 * cd])
        z[ch] = gate_c * hid
        z_ref[b, CONV_HALO + r:CONV_HALO + r + sub, :] = z[ch]
        gate_b[ch] = _dot(xb, w_in_ref[:, 0:cd])
    cw = conv_w_ref[...]
    for ch in chains:
        b, r = ch
        conv = cw[CONV_WIDTH - 1:CONV_WIDTH, :] * z[ch]
        for k in range(CONV_WIDTH - 1):
            off = CONV_HALO + r - (CONV_WIDTH - 1) + k
            conv = conv + cw[k:k + 1, :] * z_ref[b, off:off + sub, :]
        y[ch] = (gate_b[ch] * conv).astype(_BF16)
    for b in range(bsz):
        z_ref[b, 0:CONV_HALO, :] = z[(b, ts - sub)][sub - CONV_HALO:sub, :]
    for ch in chains:
        mix[ch] = _dot(y[ch], w_out_ref[...])
    for ch in chains:
        b, r = ch
        o_ref[b, r:r + sub, :] = _layer_norm(DEEPNORM_ALPHA * xf[ch] + mix[ch],
                                             ln1_g_ref[...], ln1_b_ref[...])


def _ffn_kernel(x_ref, p_ref, w1_ref, w2_ref, ln2_g_ref, ln2_b_ref, wg_ref, wp_ref, o_ref,
                *, ff_chunk):
    sub = SUB_ROWS
    d_ff = w1_ref.shape[1]
    chains = _chains(x_ref)

    xf, acc, x2, pp = {}, {}, {}, {}
    for ch in chains:
        b, r = ch
        xf[ch] = x_ref[b, r:r + sub, :]
        xb = xf[ch].astype(_BF16)
        part_sum = None
        for c in range(d_ff // ff_chunk):
            h = _dot(xb, w1_ref[:, c * ff_chunk:(c + 1) * ff_chunk])
            h = jnp.maximum(h, 0.0)
            hb = (h * h).astype(_BF16)
            part = _dot(hb, w2_ref[c * ff_chunk:(c + 1) * ff_chunk, :])
            part_sum = part if part_sum is None else part_sum + part
        acc[ch] = part_sum
    for ch in chains:
        b, r = ch
        x2[ch] = _layer_norm(DEEPNORM_ALPHA * xf[ch] + acc[ch], ln2_g_ref[...], ln2_b_ref[...])
        pp[ch] = _dot(p_ref[b, r:r + sub, :].astype(_BF16), wp_ref[...])
    for ch in chains:
        b, r = ch
        gate = jax.nn.sigmoid(_dot(x2[ch].astype(_BF16), wg_ref[...]))
        o_ref[b, r:r + sub, :] = x2[ch] + gate * pp[ch]


def _resident(shape):
    zeros = (0,) * len(shape)
    return pl.BlockSpec(shape, lambda s: zeros, pipeline_mode=pl.Buffered(1))


def _compiler_params():
    return pltpu.CompilerParams(dimension_semantics=("arbitrary",),
                                vmem_limit_bytes=VMEM_LIMIT_BYTES)


def _row(v):
    return v.reshape(1, -1)


def _act_spec(bsz, ts, d):
    return pl.BlockSpec((bsz, ts, d), lambda s: (0, s, 0))


def _even_mixer(x, w_in, pool_bd, pool_scale, sgu_g, sgu_b, sgu_w, sgu_bias, w_out, g1, b1):
    bsz, seq, d = x.shape
    ts = SEQ_TILE
    act = _act_spec(bsz, ts, d)
    args = (w_in, pool_bd, pool_scale, sgu_g, sgu_b, sgu_w, sgu_bias, w_out, g1, b1)
    pool_buf = pltpu.VMEM((bsz, POOL_BASE + ts, POOL_WIDTH), _F32)
    return pl.pallas_call(
        _even_mixer_kernel,
        grid=(seq // ts,),
        in_specs=[act] + [_resident(a.shape) for a in args],
        out_specs=act,
        out_shape=jax.ShapeDtypeStruct(x.shape, x.dtype),
        scratch_shapes=[pool_buf, pool_buf, pool_buf, pool_buf,
                        pltpu.VMEM((bsz, ts, d), _BF16)],
        compiler_params=_compiler_params(),
        name="even_mixer",
    )(x, *args)


def _odd_mixer(x, w_in, conv_w, w_out, g1, b1):
    bsz, seq, d = x.shape
    ts = SEQ_TILE
    act = _act_spec(bsz, ts, d)
    args = (w_in, conv_w, w_out, g1, b1)
    return pl.pallas_call(
        _odd_mixer_kernel,
        grid=(seq // ts,),
        in_specs=[act] + [_resident(a.shape) for a in args],
        out_specs=act,
        out_shape=jax.ShapeDtypeStruct(x.shape, x.dtype),
        scratch_shapes=[pltpu.VMEM((bsz, CONV_HALO + ts, w_out.shape[0]), _F32)],
        compiler_params=_compiler_params(),
        name="odd_mixer",
    )(x, *args)


def _ffn(x, p, layer, w1, w2, g2, b2, wg, wp):
    bsz, seq, d = x.shape
    ts = SEQ_TILE
    act = _act_spec(bsz, ts, d)
    p_spec = pl.BlockSpec((None, bsz, ts, p.shape[-1]), lambda s: (layer, 0, s, 0))
    args = (w1, w2, g2, b2, wg, wp)
    return pl.pallas_call(
        functools.partial(_ffn_kernel, ff_chunk=d),
        grid=(seq // ts,),
        in_specs=[act, p_spec] + [_resident(a.shape) for a in args],
        out_specs=act,
        out_shape=jax.ShapeDtypeStruct(x.shape, x.dtype),
        compiler_params=_compiler_params(),
        name=f"ffn_{layer}",
    )(x, p, *args)


def _block_diag_pairs(pool_w):
    g = POOL_GROUP_DIM
    z = jnp.zeros((g, g), pool_w.dtype)
    pairs = []
    for k in range(0, len(POOL_WINDOWS), 2):
        top = jnp.concatenate([pool_w[k], z], axis=1)
        bot = jnp.concatenate([z, pool_w[k + 1]], axis=1)
        pairs.append(jnp.concatenate([top, bot], axis=0))
    return jnp.stack(pairs)


def kernel(x, p, ev_w_in, ev_pool_w, ev_pool_scale, ev_sgu_ln_g, ev_sgu_ln_b, ev_sgu_w, ev_sgu_b,
           ev_w_out, od_w_in, od_conv_w, od_w_out, ln1_g, ln1_b, ffn_w1, ffn_w2, ln2_g, ln2_b,
           ple_gate_w, ple_w):
    assert x.shape[1] % SEQ_TILE == 0 and SEQ_TILE % SUB_ROWS == 0 and SUB_ROWS % CHUNK == 0
    bf = lambda w: w.astype(_BF16)
    for i in range(DEPTH):
        j = i // 2
        if i % 2 == 0:
            sgu_bias = jnp.repeat(ev_sgu_b[j].T, SGU_HEAD_DIM, axis=1)
            x = _even_mixer(x, bf(ev_w_in[j]), bf(_block_diag_pairs(ev_pool_w[j])),
                            _row(ev_pool_scale[j]), _row(ev_sgu_ln_g[j]), _row(ev_sgu_ln_b[j]),
                            ev_sgu_w[j], sgu_bias, bf(ev_w_out[j]),
                            _row(ln1_g[i]), _row(ln1_b[i]))
        else:
            x = _odd_mixer(x, bf(od_w_in[j]), od_conv_w[j], bf(od_w_out[j]),
                           _row(ln1_g[i]), _row(ln1_b[i]))
        x = _ffn(x, p, i, bf(ffn_w1[i]), bf(ffn_w2[i]), _row(ln2_g[i]), _row(ln2_b[i]),
                 bf(ple_gate_w[i]), bf(ple_w[i]))
    return x
```

```python
import functools

import jax
import jax.numpy as jnp
from jax import lax
from jax.experimental import pallas as pl
from jax.experimental.pallas import tpu as pltpu

POOL_WINDOWS = (2, 4, 8, 16)
POOL_GROUP_DIM = 128
POOL_WIDTH = POOL_GROUP_DIM * len(POOL_WINDOWS)
SGU_HEADS = 4
SGU_HEAD_DIM = 128
SGU_WIDTH = SGU_HEADS * SGU_HEAD_DIM
CHUNK = 128
CONV_WIDTH = 3
LN_EPS = 1e-5
DEPTH = 2
DEEPNORM_ALPHA = (2.0 * DEPTH) ** 0.25

SUBLANES = 8
POOL_HALO = max(POOL_WINDOWS)
POOL_PAD = SUBLANES
POOL_BASE = POOL_PAD + POOL_HALO
CONV_HALO = SUBLANES

SEQ_TILE = 256
SUB_ROWS = 256
VMEM_LIMIT_BYTES = 60 * 1024 * 1024

_BF16 = jnp.bfloat16
_F32 = jnp.float32


def _dot(a, b):
    return jnp.dot(a, b, preferred_element_type=_F32)


def _layer_norm(x, g, b):
    mu = jnp.mean(x, axis=-1, keepdims=True)
    xc = x - mu
    var = jnp.mean(xc * xc, axis=-1, keepdims=True)
    return xc * lax.rsqrt(var + LN_EPS) * g + b


def _chains(x_ref):
    bsz, ts, _ = x_ref.shape
    return [(b, r) for b in range(bsz) for r in range(0, ts, SUB_ROWS)]


def _even_mixer_kernel(x_ref, w_in_ref, pool_bd_ref, pool_scale_ref, sgu_g_ref, sgu_b_ref,
                       sgu_w_ref, sgu_bias_ref, w_out_ref, ln1_g_ref, ln1_b_ref,
                       o_ref, ext_ref, lvl2_ref, lvl4_ref, lvl8_ref, y_ref):
    s = pl.program_id(0)
    bsz, ts, _ = x_ref.shape
    sub, g, d, base = SUB_ROWS, POOL_GROUP_DIM, SGU_HEAD_DIM, POOL_BASE
    chains = _chains(x_ref)

    @pl.when(s == 0)
    def _():
        for b in range(bsz):
            ext_ref[b, 0:base, :] = jnp.zeros((base, POOL_WIDTH), _F32)
            lvl2_ref[b, 0:POOL_PAD, :] = jnp.zeros((POOL_PAD, POOL_WIDTH), _F32)
            lvl4_ref[b, 0:POOL_PAD, :] = jnp.zeros((POOL_PAD, POOL_WIDTH), _F32)

    xf, a, u, v, pooled, vn, mix = {}, {}, {}, {}, {}, {}, {}
    for ch in chains:
        b, r = ch
        xf[ch] = x_ref[b, r:r + sub, :]
        xb = xf[ch].astype(_BF16)
        a[ch] = _dot(xb, w_in_ref[:, 0:POOL_WIDTH])
        ext_ref[b, base + r:base + r + sub, :] = a[ch]
        u[ch] = _dot(xb, w_in_ref[:, POOL_WIDTH:POOL_WIDTH + SGU_WIDTH])
        v[ch] = _dot(xb, w_in_ref[:, POOL_WIDTH + SGU_WIDTH:])

    for ch in chains:
        b, r = ch
        lo = POOL_PAD if r == 0 else base + r
        hi = base + r + sub
        own = slice(base + r, hi)
        lvl2_ref[b, lo:hi, :] = ext_ref[b, lo:hi, :] + ext_ref[b, lo - 1:hi - 1, :]
        lvl4_ref[b, lo:hi, g:] = lvl2_ref[b, lo:hi, g:] + lvl2_ref[b, lo - 2:hi - 2, g:]
        lvl8_ref[b, lo:hi, 2 * g:] = lvl4_ref[b, lo:hi, 2 * g:] + lvl4_ref[b, lo - 4:hi - 4, 2 * g:]
        sums = (
            lvl2_ref[b, own, 0:g],
            lvl4_ref[b, own, g:2 * g],
            lvl8_ref[b, own, 2 * g:3 * g],
            lvl8_ref[b, own, 3 * g:4 * g] + lvl8_ref[b, base + r - 8:hi - 8, 3 * g:4 * g],
        )
        t1 = lax.broadcasted_iota(jnp.int32, (sub, g), 0) + (s * ts + r + 1)
        parts = []
        for gi, w in enumerate(POOL_WINDOWS):
            cnt = jnp.minimum(t1, w).astype(_F32)
            parts.append((sums[gi] / cnt - a[ch][:, gi * g:(gi + 1) * g]).astype(_BF16))
        pooled[ch] = parts
        vn[ch] = _layer_norm(v[ch], sgu_g_ref[...], sgu_b_ref[...]).astype(_BF16)

    for b in range(bsz):
        ext_ref[b, POOL_PAD:base, :] = a[(b, ts - sub)][sub - POOL_HALO:sub, :]

    scale = pool_scale_ref[...]
    n_chunks = sub // CHUNK
    tri = (lax.broadcasted_iota(jnp.int32, (CHUNK, CHUNK), 0)
           >= lax.broadcasted_iota(jnp.int32, (CHUNK, CHUNK), 1))
    w_tri = [jnp.where(tri, sgu_w_ref[h], 0.0).astype(_BF16) for h in range(SGU_HEADS)]
    for ch in chains:
        b, r = ch
        for pair in range(2):
            pb = jnp.concatenate(pooled[ch][2 * pair:2 * pair + 2], axis=1)
            ya = _dot(pb, pool_bd_ref[pair]) * scale[:, 2 * g * pair:2 * g * (pair + 1)]
            y_ref[b, r:r + sub, 2 * g * pair:2 * g * (pair + 1)] = ya.astype(_BF16)
        for h in range(SGU_HEADS):
            rhs = jnp.concatenate(
                [vn[ch][c * CHUNK:(c + 1) * CHUNK, h * d:(h + 1) * d] for c in range(n_chunks)],
                axis=1)
            mixed = _dot(w_tri[h], rhs)
            bias_h = sgu_bias_ref[:, h * d:(h + 1) * d]
            for c in range(n_chunks):
                m = mixed[:, c * d:(c + 1) * d] + bias_h
                yb = u[ch][c * CHUNK:(c + 1) * CHUNK, h * d:(h + 1) * d] * m
                y_ref[b, r + c * CHUNK:r + (c + 1) * CHUNK,
                      POOL_WIDTH + h * d:POOL_WIDTH + (h + 1) * d] = yb.astype(_BF16)

    for ch in chains:
        b, r = ch
        mix[ch] = _dot(y_ref[b, r:r + sub, :], w_out_ref[...])
    for ch in chains:
        b, r = ch
        o_ref[b, r:r + sub, :] = _layer_norm(DEEPNORM_ALPHA * xf[ch] + mix[ch],
                                             ln1_g_ref[...], ln1_b_ref[...])


def _odd_mixer_kernel(x_ref, w_in_ref, conv_w_ref, w_out_ref, ln1_g_ref, ln1_b_ref,
                      o_ref, z_ref):
    s = pl.program_id(0)
    bsz, ts, _ = x_ref.shape
    sub = SUB_ROWS
    cd = w_out_ref.shape[0]
    chains = _chains(x_ref)

    @pl.when(s == 0)
    def _():
        for b in range(bsz):
            z_ref[b, 0:CONV_HALO, :] = jnp.zeros((CONV_HALO, cd), _F32)

    xf, z, gate_b, y, mix = {}, {}, {}, {}, {}
    for ch in chains:
        b, r = ch
        xf[ch] = x_ref[b, r:r + sub, :]
        xb = xf[ch].astype(_BF16)
        gate_c = _dot(xb, w_in_ref[:, cd:2 * cd])
        hid = _dot(xb, w_in_ref[:, 2 * cd:3 * cd])
        z[ch] = gate_c * hid
        z_ref[b, CONV_HALO + r:CONV_HALO + r + sub, :] = z[ch]
        gate_b[ch] = _dot(xb, w_in_ref[:, 0:cd])
    cw = conv_w_ref[...]
    for ch in chains:
        b, r = ch
        conv = cw[CONV_WIDTH - 1:CONV_WIDTH, :] * z[ch]
        for k in range(CONV_WIDTH - 1):
            off = CONV_HALO + r - (CONV_WIDTH - 1) + k
            conv = conv + cw[k:k + 1, :] * z_ref[b, off:off + sub, :]
        y[ch] = (gate_b[ch] * conv).astype(_BF16)
    for b in range(bsz):
        z_ref[b, 0:CONV_HALO, :] = z[(b, ts - sub)][sub - CONV_HALO:sub, :]
    for ch in chains:
        mix[ch] = _dot(y[ch], w_out_ref[...])
    for ch in chains:
        b, r = ch
        o_ref[b, r:r + sub, :] = _layer_norm(DEEPNORM_ALPHA * xf[ch] + mix[ch],
                                             ln1_g_ref[...], ln1_b_ref[...])


def _ffn_kernel(x_ref, p_ref, w1_ref, w2_ref, ln2_g_ref, ln2_b_ref, wg_ref, wp_ref, o_ref,
                *, ff_chunk):
    sub = SUB_ROWS
    d_ff = w1_ref.shape[1]
    chains = _chains(x_ref)

    def mlp(ch):
        b, r = ch
        xf = x_ref[b, r:r + sub, :]
        xb = xf.astype(_BF16)
        acc = None
        for c in range(d_ff // ff_chunk):
            h = _dot(xb, w1_ref[:, c * ff_chunk:(c + 1) * ff_chunk])
            h = jnp.maximum(h, 0.0)
            hb = (h * h).astype(_BF16)
            part = _dot(hb, w2_ref[c * ff_chunk:(c + 1) * ff_chunk, :])
            acc = part if acc is None else acc + part
        return _layer_norm(DEEPNORM_ALPHA * xf + acc, ln2_g_ref[...], ln2_b_ref[...])

    def embed_gate(ch, x2):
        b, r = ch
        gate = jax.nn.sigmoid(_dot(x2.astype(_BF16), wg_ref[...]))
        pp = _dot(p_ref[b, r:r + sub, :].astype(_BF16), wp_ref[...])
        o_ref[b, r:r + sub, :] = x2 + gate * pp

    pending = None
    for ch in chains:
        x2 = mlp(ch)
        if pending is not None:
            embed_gate(*pending)
        pending = (ch, x2)
    embed_gate(*pending)


def _resident(shape):
    zeros = (0,) * len(shape)
    return pl.BlockSpec(shape, lambda s: zeros, pipeline_mode=pl.Buffered(1))


def _compiler_params():
    return pltpu.CompilerParams(dimension_semantics=("arbitrary",),
                                vmem_limit_bytes=VMEM_LIMIT_BYTES)


def _row(v):
    return v.reshape(1, -1)


def _act_spec(bsz, ts, d):
    return pl.BlockSpec((bsz, ts, d), lambda s: (0, s, 0))


def _with_weight_casts(body, n_in, n_cast):
    def kern(*refs):
        ins, srcs = refs[:n_in], refs[n_in:n_in + n_cast]
        out = refs[n_in + n_cast]
        dsts = refs[n_in + n_cast + 1:n_in + 2 * n_cast + 1]
        scratch = refs[n_in + 2 * n_cast + 1:]
        body(*ins, out, *scratch)
        for src, dst in zip(srcs, dsts):
            dst[...] = src[...].astype(_BF16)
    return kern


def _call(name, body, x, streamed, resident, scratch, casts):
    bsz, seq, d = x.shape
    ts = SEQ_TILE
    steps = seq // ts
    act = _act_spec(bsz, ts, d)
    cast_in, cast_out, cast_shapes = [], [], []
    for w, layer in casts:
        _, rows, cols = w.shape
        slab = max(rows // steps, 2 * SUBLANES)
        last = rows // slab - 1
        assert slab * (last + 1) == rows and slab % (2 * SUBLANES) == 0 and last < steps
        cast_in.append(pl.BlockSpec((None, slab, cols),
                                    lambda s, layer=layer, last=last: (layer, jnp.minimum(s, last), 0)))
        cast_out.append(pl.BlockSpec((slab, cols), lambda s, last=last: (jnp.minimum(s, last), 0)))
        cast_shapes.append(jax.ShapeDtypeStruct((rows, cols), _BF16))
    n_in = 1 + len(streamed) + len(resident)
    outs = pl.pallas_call(
        _with_weight_casts(body, n_in, len(casts)),
        grid=(steps,),
        in_specs=([act] + [spec for _, spec in streamed]
                  + [_resident(a.shape) for a in resident] + cast_in),
        out_specs=[act] + cast_out,
        out_shape=[jax.ShapeDtypeStruct(x.shape, x.dtype)] + cast_shapes,
        scratch_shapes=scratch,
        compiler_params=_compiler_params(),
        name=name,
    )(x, *[a for a, _ in streamed], *resident, *[w for w, _ in casts])
    return outs[0], outs[1:]


def _even_mixer(x, weights, casts):
    bsz, _, d = x.shape
    pool_buf = pltpu.VMEM((bsz, POOL_BASE + SEQ_TILE, POOL_WIDTH), _F32)
    scratch = [pool_buf, pool_buf, pool_buf, pool_buf, pltpu.VMEM((bsz, SEQ_TILE, d), _BF16)]
    return _call("even_mixer", _even_mixer_kernel, x, [], weights, scratch, casts)


def _odd_mixer(x, weights, casts):
    bsz = x.shape[0]
    conv_dim = weights[2].shape[0]
    scratch = [pltpu.VMEM((bsz, CONV_HALO + SEQ_TILE, conv_dim), _F32)]
    return _call("odd_mixer", _odd_mixer_kernel, x, [], weights, scratch, casts)


def _ffn(x, p, layer, weights, casts):
    bsz, _, d = x.shape
    p_spec = pl.BlockSpec((None, bsz, SEQ_TILE, p.shape[-1]), lambda s: (layer, 0, s, 0))
    body = functools.partial(_ffn_kernel, ff_chunk=d)
    return _call(f"ffn_{layer}", body, x, [(p, p_spec)], weights, [], casts)


def _block_diag_pairs(pool_w):
    g = POOL_GROUP_DIM
    z = jnp.zeros((g, g), pool_w.dtype)
    pairs = []
    for k in range(0, len(POOL_WINDOWS), 2):
        top = jnp.concatenate([pool_w[k], z], axis=1)
        bot = jnp.concatenate([z, pool_w[k + 1]], axis=1)
        pairs.append(jnp.concatenate([top, bot], axis=0))
    return jnp.stack(pairs)


def kernel(x, p, ev_w_in, ev_pool_w, ev_pool_scale, ev_sgu_ln_g, ev_sgu_ln_b, ev_sgu_w, ev_sgu_b,
           ev_w_out, od_w_in, od_conv_w, od_w_out, ln1_g, ln1_b, ffn_w1, ffn_w2, ln2_g, ln2_b,
           ple_gate_w, ple_w):
    assert x.shape[1] % SEQ_TILE == 0 and SEQ_TILE % SUB_ROWS == 0 and SUB_ROWS % CHUNK == 0
    bf = lambda w: w.astype(_BF16)

    def mixer_casts(i):
        if i >= DEPTH:
            return []
        return ([(ev_w_in, i // 2), (ev_w_out, i // 2)] if i % 2 == 0
                else [(od_w_in, i // 2), (od_w_out, i // 2)])

    w_in, w_out = bf(ev_w_in[0]), bf(ev_w_out[0])
    for i in range(DEPTH):
        j = i // 2
        ffn_casts = [(ffn_w1, i), (ffn_w2, i), (ple_gate_w, i), (ple_w, i)]
        if i % 2 == 0:
            sgu_bias = jnp.repeat(ev_sgu_b[j].T, SGU_HEAD_DIM, axis=1)
            weights = (w_in, bf(_block_diag_pairs(ev_pool_w[j])), _row(ev_pool_scale[j]),
                       _row(ev_sgu_ln_g[j]), _row(ev_sgu_ln_b[j]), ev_sgu_w[j], sgu_bias,
                       w_out, _row(ln1_g[i]), _row(ln1_b[i]))
            x, (w1, w2, wg, wp) = _even_mixer(x, weights, ffn_casts)
        else:
            weights = (w_in, od_conv_w[j], w_out, _row(ln1_g[i]), _row(ln1_b[i]))
            x, (w1, w2, wg, wp) = _odd_mixer(x, weights, ffn_casts)
        weights = (w1, w2, _row(ln2_g[i]), _row(ln2_b[i]), wg, wp)
        x, next_mixer = _ffn(x, p, i, weights, mixer_casts(i + 1))
        if next_mixer:
            w_in, w_out = next_mixer
    return x
```

```python
import jax
import jax.numpy as jnp
from jax import lax
from jax.experimental import pallas as pl
from jax.experimental.pallas import tpu as pltpu

POOL_WINDOWS = (2, 4, 8, 16)
POOL_GROUP_DIM = 128
POOL_WIDTH = POOL_GROUP_DIM * len(POOL_WINDOWS)
SGU_HEADS = 4
SGU_HEAD_DIM = 128
SGU_WIDTH = SGU_HEADS * SGU_HEAD_DIM
CHUNK = 128
CONV_WIDTH = 3
LN_EPS = 1e-5
DEPTH = 2
DEEPNORM_ALPHA = (2.0 * DEPTH) ** 0.25

SUBLANES = 8
POOL_HALO = max(POOL_WINDOWS)
POOL_PAD = SUBLANES
POOL_BASE = POOL_PAD + POOL_HALO
CONV_HALO = SUBLANES

SEQ_TILE = 512
SUB_ROWS = 256
VMEM_LIMIT_BYTES = 60 * 1024 * 1024

_BF16 = jnp.bfloat16
_F32 = jnp.float32


def _dot(a, b):
    return jnp.dot(a, b, preferred_element_type=_F32)


def _layer_norm(x, g, b):
    mu = jnp.mean(x, axis=-1, keepdims=True)
    xc = x - mu
    var = jnp.mean(xc * xc, axis=-1, keepdims=True)
    return xc * lax.rsqrt(var + LN_EPS) * g + b


def _chains(x_ref):
    bsz, ts, _ = x_ref.shape
    return [(b, r) for b in range(bsz) for r in range(0, ts, SUB_ROWS)]


def _project_and_norm(chains, lhs, xf, w_out_ref, g_ref, b_ref, o_ref):
    mix = {ch: _dot(lhs(ch), w_out_ref[...]) for ch in chains}
    for ch in chains:
        b, r = ch
        o_ref[b, r:r + SUB_ROWS, :] = _layer_norm(DEEPNORM_ALPHA * xf[ch] + mix[ch],
                                                  g_ref[...], b_ref[...])


def _even_mixer_kernel(x_ref, w_in_f32_ref, pool_w_ref, pool_scale_ref, sgu_g_ref, sgu_b_ref,
                       sgu_w_ref, sgu_bias_ref, w_out_f32_ref, ln1_g_ref, ln1_b_ref,
                       o_ref, w_in_ref, w_out_ref, pool_bd_ref, w_tri_ref,
                       ext_ref, lvl2_ref, lvl4_ref, lvl8_ref, y_ref):
    s = pl.program_id(0)
    bsz, ts, _ = x_ref.shape
    sub, g, d, base = SUB_ROWS, POOL_GROUP_DIM, SGU_HEAD_DIM, POOL_BASE
    chains = _chains(x_ref)

    @pl.when(s == 0)
    def _():
        w_in_ref[...] = w_in_f32_ref[...].astype(_BF16)
        w_out_ref[...] = w_out_f32_ref[...].astype(_BF16)
        pool_bd_ref[...] = jnp.zeros(pool_bd_ref.shape, _BF16)
        for k in range(len(POOL_WINDOWS)):
            off = (k % 2) * g
            pool_bd_ref[k // 2, off:off + g, off:off + g] = pool_w_ref[k].astype(_BF16)
        tri = (lax.broadcasted_iota(jnp.int32, (CHUNK, CHUNK), 0)
               >= lax.broadcasted_iota(jnp.int32, (CHUNK, CHUNK), 1))
        for h in range(SGU_HEADS):
            w_tri_ref[h] = jnp.where(tri, sgu_w_ref[h], 0.0).astype(_BF16)
        for b in range(bsz):
            ext_ref[b, 0:base, :] = jnp.zeros((base, POOL_WIDTH), _F32)
            lvl2_ref[b, 0:POOL_PAD, :] = jnp.zeros((POOL_PAD, POOL_WIDTH), _F32)
            lvl4_ref[b, 0:POOL_PAD, :] = jnp.zeros((POOL_PAD, POOL_WIDTH), _F32)

    xf, a, u, v, pooled, vn = {}, {}, {}, {}, {}, {}
    for ch in chains:
        b, r = ch
        xf[ch] = x_ref[b, r:r + sub, :]
        xb = xf[ch].astype(_BF16)
        a[ch] = _dot(xb, w_in_ref[:, 0:POOL_WIDTH])
        ext_ref[b, base + r:base + r + sub, :] = a[ch]
        u[ch] = _dot(xb, w_in_ref[:, POOL_WIDTH:POOL_WIDTH + SGU_WIDTH])
        v[ch] = _dot(xb, w_in_ref[:, POOL_WIDTH + SGU_WIDTH:])

    for ch in chains:
        b, r = ch
        lo = POOL_PAD if r == 0 else base + r
        hi = base + r + sub
        own = slice(base + r, hi)
        lvl2_ref[b, lo:hi, :] = ext_ref[b, lo:hi, :] + ext_ref[b, lo - 1:hi - 1, :]
        lvl4_ref[b, lo:hi, g:] = lvl2_ref[b, lo:hi, g:] + lvl2_ref[b, lo - 2:hi - 2, g:]
        lvl8_ref[b, lo:hi, 2 * g:] = lvl4_ref[b, lo:hi, 2 * g:] + lvl4_ref[b, lo - 4:hi - 4, 2 * g:]
        sums = (
            lvl2_ref[b, own, 0:g],
            lvl4_ref[b, own, g:2 * g],
            lvl8_ref[b, own, 2 * g:3 * g],
            lvl8_ref[b, own, 3 * g:4 * g] + lvl8_ref[b, base + r - 8:hi - 8, 3 * g:4 * g],
        )
        t1 = lax.broadcasted_iota(jnp.int32, (sub, g), 0) + (s * ts + r + 1)
        parts = []
        for gi, w in enumerate(POOL_WINDOWS):
            cnt = jnp.minimum(t1, w).astype(_F32)
            parts.append((sums[gi] / cnt - a[ch][:, gi * g:(gi + 1) * g]).astype(_BF16))
        pooled[ch] = parts
        vn[ch] = _layer_norm(v[ch], sgu_g_ref[...], sgu_b_ref[...]).astype(_BF16)

    for b in range(bsz):
        ext_ref[b, POOL_PAD:base, :] = a[(b, ts - sub)][sub - POOL_HALO:sub, :]

    scale = pool_scale_ref[...]
    n_chunks = sub // CHUNK
    for ch in chains:
        b, r = ch
        for pair in range(2):
            pb = jnp.concatenate(pooled[ch][2 * pair:2 * pair + 2], axis=1)
            ya = _dot(pb, pool_bd_ref[pair]) * scale[:, 2 * g * pair:2 * g * (pair + 1)]
            y_ref[b, r:r + sub, 2 * g * pair:2 * g * (pair + 1)] = ya.astype(_BF16)
        for h in range(SGU_HEADS):
            rhs = jnp.concatenate(
                [vn[ch][c * CHUNK:(c + 1) * CHUNK, h * d:(h + 1) * d] for c in range(n_chunks)],
                axis=1)
            mixed = _dot(w_tri_ref[h], rhs)
            bias_h = sgu_bias_ref[:, h * d:(h + 1) * d]
            for c in range(n_chunks):
                m = mixed[:, c * d:(c + 1) * d] + bias_h
                yb = u[ch][c * CHUNK:(c + 1) * CHUNK, h * d:(h + 1) * d] * m
                y_ref[b, r + c * CHUNK:r + (c + 1) * CHUNK,
                      POOL_WIDTH + h * d:POOL_WIDTH + (h + 1) * d] = yb.astype(_BF16)

    _project_and_norm(chains, lambda ch: y_ref[ch[0], ch[1]:ch[1] + sub, :], xf,
                      w_out_ref, ln1_g_ref, ln1_b_ref, o_ref)


def _odd_mixer_kernel(x_ref, w_in_ref, conv_w_ref, w_out_ref, ln1_g_ref, ln1_b_ref,
                      o_ref, z_ref):
    s = pl.program_id(0)
    bsz, ts, _ = x_ref.shape
    sub = SUB_ROWS
    cd = w_out_ref.shape[0]
    chains = _chains(x_ref)

    @pl.when(s == 0)
    def _():
        for b in range(bsz):
            z_ref[b, 0:CONV_HALO, :] = jnp.zeros((CONV_HALO, cd), _F32)

    xf, z, gate_b, y = {}, {}, {}, {}
    for ch in chains:
        b, r = ch
        xf[ch] = x_ref[b, r:r + sub, :]
        xb = xf[ch].astype(_BF16)
        gate_c = _dot(xb, w_in_ref[:, cd:2 * cd])
        hid = _dot(xb, w_in_ref[:, 2 * cd:3 * cd])
        z[ch] = gate_c * hid
        z_ref[b, CONV_HALO + r:CONV_HALO + r + sub, :] = z[ch]
        gate_b[ch] = _dot(xb, w_in_ref[:, 0:cd])
    cw = conv_w_ref[...]
    for ch in chains:
        b, r = ch
        conv = cw[CONV_WIDTH - 1:CONV_WIDTH, :] * z[ch]
        for k in range(CONV_WIDTH - 1):
            off = CONV_HALO + r - (CONV_WIDTH - 1) + k
            conv = conv + cw[k:k + 1, :] * z_ref[b, off:off + sub, :]
        y[ch] = (gate_b[ch] * conv).astype(_BF16)
    for b in range(bsz):
        z_ref[b, 0:CONV_HALO, :] = z[(b, ts - sub)][sub - CONV_HALO:sub, :]
    _project_and_norm(chains, lambda ch: y[ch], xf, w_out_ref, ln1_g_ref, ln1_b_ref, o_ref)


def _ffn_kernel(x_ref, p_ref, w1_ref, w2_ref, ln2_g_ref, ln2_b_ref, wg_ref, wp_ref, o_ref):
    sub = SUB_ROWS
    d_ff = w1_ref.shape[1]
    ff_chunk = w1_ref.shape[0]
    chains = _chains(x_ref)

    def mlp(ch):
        b, r = ch
        xf = x_ref[b, r:r + sub, :]
        xb = xf.astype(_BF16)
        hidden = []
        for c in range(d_ff // ff_chunk):
            h = _dot(xb, w1_ref[:, c * ff_chunk:(c + 1) * ff_chunk])
            h = jnp.maximum(h, 0.0)
            hidden.append((h * h).astype(_BF16))
        acc = _dot(jnp.concatenate(hidden, axis=1), w2_ref[...])
        return _layer_norm(DEEPNORM_ALPHA * xf + acc, ln2_g_ref[...], ln2_b_ref[...])

    def embed_gate(ch, x2):
        b, r = ch
        gate = jax.nn.sigmoid(_dot(x2.astype(_BF16), wg_ref[...]))
        pp = _dot(p_ref[b, r:r + sub, :].astype(_BF16), wp_ref[...])
        o_ref[b, r:r + sub, :] = x2 + gate * pp

    pending = None
    for ch in chains:
        x2 = mlp(ch)
        if pending is not None:
            embed_gate(*pending)
        pending = (ch, x2)
    embed_gate(*pending)


def _with_weight_casts(body, n_in, n_cast):
    def kern(*refs):
        ins, srcs = refs[:n_in], refs[n_in:n_in + n_cast]
        out = refs[n_in + n_cast]
        dsts = refs[n_in + n_cast + 1:n_in + 2 * n_cast + 1]
        scratch = refs[n_in + 2 * n_cast + 1:]
        body(*ins, out, *scratch)
        for src, dst in zip(srcs, dsts):
            dst[...] = src[...].astype(_BF16)
    return kern


def _resident(arr, index=None):
    if index is None:
        zeros = (0,) * arr.ndim
        return pl.BlockSpec(arr.shape, lambda s: zeros, pipeline_mode=pl.Buffered(1))
    zeros = (0,) * (arr.ndim - 1)
    return pl.BlockSpec((None,) + arr.shape[1:], lambda s: (index,) + zeros,
                        pipeline_mode=pl.Buffered(1))


def _call(name, body, x, streamed, resident, scratch, casts):
    bsz, seq, d = x.shape
    ts = SEQ_TILE
    steps = seq // ts
    act = pl.BlockSpec((bsz, ts, d), lambda s: (0, s, 0))
    cast_in, cast_out, cast_shapes = [], [], []
    for w, idx in casts:
        _, rows, cols = w.shape
        slab = max(rows // steps, 2 * SUBLANES)
        last = rows // slab - 1
        assert slab * (last + 1) == rows and slab % (2 * SUBLANES) == 0 and last < steps
        cast_in.append(pl.BlockSpec((None, slab, cols),
                                    lambda s, idx=idx, last=last: (idx, jnp.minimum(s, last), 0)))
        cast_out.append(pl.BlockSpec((slab, cols), lambda s, last=last: (jnp.minimum(s, last), 0)))
        cast_shapes.append(jax.ShapeDtypeStruct((rows, cols), _BF16))
    n_in = 1 + len(streamed) + len(resident)
    outs = pl.pallas_call(
        _with_weight_casts(body, n_in, len(casts)),
        grid=(steps,),
        in_specs=([act] + [spec for _, spec in streamed]
                  + [_resident(a, idx) for a, idx in resident] + cast_in),
        out_specs=[act] + cast_out,
        out_shape=[jax.ShapeDtypeStruct(x.shape, x.dtype)] + cast_shapes,
        scratch_shapes=scratch,
        compiler_params=pltpu.CompilerParams(dimension_semantics=("arbitrary",),
                                             vmem_limit_bytes=VMEM_LIMIT_BYTES),
        name=name,
    )(x, *[a for a, _ in streamed], *[a for a, _ in resident], *[w for w, _ in casts])
    return outs[0], outs[1:]


def _rows3(v):
    return v.reshape(v.shape[0], 1, v.shape[1])


def kernel(x, p, ev_w_in, ev_pool_w, ev_pool_scale, ev_sgu_ln_g, ev_sgu_ln_b, ev_sgu_w, ev_sgu_b,
           ev_w_out, od_w_in, od_conv_w, od_w_out, ln1_g, ln1_b, ffn_w1, ffn_w2, ln2_g, ln2_b,
           ple_gate_w, ple_w):
    assert x.shape[1] % SEQ_TILE == 0 and SEQ_TILE % SUB_ROWS == 0 and SUB_ROWS % CHUNK == 0
    assert ln1_g.shape[0] == DEPTH and ev_w_in.shape[0] == 1 and od_w_in.shape[0] == 1
    bsz, _, d = x.shape
    ln1_g, ln1_b, ln2_g, ln2_b = map(_rows3, (ln1_g, ln1_b, ln2_g, ln2_b))
    p_spec = lambda i: pl.BlockSpec((None, bsz, SEQ_TILE, p.shape[-1]), lambda s: (i, 0, s, 0))
    ffn_casts = lambda i: [(ffn_w1, i), (ffn_w2, i), (ple_gate_w, i), (ple_w, i)]

    sgu_bias = jnp.repeat(ev_sgu_b[0].T, SGU_HEAD_DIM, axis=1)
    resident = [(ev_w_in, 0), (ev_pool_w, 0), (_rows3(ev_pool_scale), 0),
                (_rows3(ev_sgu_ln_g), 0), (_rows3(ev_sgu_ln_b), 0), (ev_sgu_w, 0),
                (sgu_bias, None), (ev_w_out, 0), (ln1_g, 0), (ln1_b, 0)]
    pool_buf = pltpu.VMEM((bsz, POOL_BASE + SEQ_TILE, POOL_WIDTH), _F32)
    scratch = [pltpu.VMEM(ev_w_in.shape[1:], _BF16), pltpu.VMEM(ev_w_out.shape[1:], _BF16),
               pltpu.VMEM((2, 2 * POOL_GROUP_DIM, 2 * POOL_GROUP_DIM), _BF16),
               pltpu.VMEM((SGU_HEADS, CHUNK, CHUNK), _BF16),
               pool_buf, pool_buf, pool_buf, pool_buf, pltpu.VMEM((bsz, SEQ_TILE, d), _BF16)]
    x, (w1, w2, wg, wp) = _call("even_mixer", _even_mixer_kernel, x, [], resident, scratch,
                                ffn_casts(0))
    resident = [(w1, None), (w2, None), (ln2_g, 0), (ln2_b, 0), (wg, None), (wp, None)]
    x, (w_in, w_out) = _call("ffn_0", _ffn_kernel, x, [(p, p_spec(0))], resident, [],
                             [(od_w_in, 0), (od_w_out, 0)])

    resident = [(w_in, None), (od_conv_w, 0), (w_out, None), (ln1_g, 1), (ln1_b, 1)]
    scratch = [pltpu.VMEM((bsz, CONV_HALO + SEQ_TILE, w_out.shape[0]), _F32)]
    x, (w1, w2, wg, wp) = _call("odd_mixer", _odd_mixer_kernel, x, [], resident, scratch,
                                ffn_casts(1))
    resident = [(w1, None), (w2, None), (ln2_g, 1), (ln2_b, 1), (wg, None), (wp, None)]
    x, _ = _call("ffn_1", _ffn_kernel, x, [(p, p_spec(1))], resident, [], [])
    return x
```

```python
import functools

import jax
import jax.numpy as jnp
from jax import lax
from jax.experimental import pallas as pl
from jax.experimental.pallas import tpu as pltpu

POOL_WINDOWS = (2, 4, 8, 16)
POOL_GROUP_DIM = 128
POOL_WIDTH = POOL_GROUP_DIM * len(POOL_WINDOWS)
SGU_HEADS = 4
SGU_HEAD_DIM = 128
SGU_WIDTH = SGU_HEADS * SGU_HEAD_DIM
CHUNK = 128
CONV_WIDTH = 3
LN_EPS = 1e-5
DEPTH = 2
DEEPNORM_ALPHA = (2.0 * DEPTH) ** 0.25

SUBLANES = 8
POOL_HALO = max(POOL_WINDOWS)
POOL_PAD = SUBLANES
POOL_BASE = POOL_PAD + POOL_HALO
CONV_HALO = SUBLANES

SEQ_TILE = 512
SUB_ROWS = 256
VMEM_LIMIT_BYTES = 60 * 1024 * 1024

_BF16 = jnp.bfloat16
_F32 = jnp.float32


def _dot(a, b):
    return jnp.dot(a, b, preferred_element_type=_F32)


def _layer_norm(x, g, b):
    mu = jnp.mean(x, axis=-1, keepdims=True)
    xc = x - mu
    var = jnp.mean(xc * xc, axis=-1, keepdims=True)
    return xc * lax.rsqrt(var + LN_EPS) * g + b


def _chains(x_ref):
    bsz, ts, _ = x_ref.shape
    return [(b, r) for b in range(bsz) for r in range(0, ts, SUB_ROWS)]


def _layer_row(ref, layer):
    return ref[layer:layer + 1, :]


def _project_and_norm(chains, lhs, xf, w_out_ref, gain, bias, o_ref):
    mix = {ch: _dot(lhs(ch), w_out_ref[...]) for ch in chains}
    for ch in chains:
        b, r = ch
        o_ref[b, r:r + SUB_ROWS, :] = _layer_norm(DEEPNORM_ALPHA * xf[ch] + mix[ch], gain, bias)


def _even_mixer_kernel(x_ref, w_in_f32_ref, pool_w_ref, pool_scale_ref, sgu_g_ref, sgu_b_ref,
                       sgu_w_ref, sgu_bias_ref, w_out_f32_ref, ln1_g_ref, ln1_b_ref,
                       o_ref, w_in_ref, w_out_ref, pool_bd_ref, w_tri_ref,
                       ext_ref, lvl2_ref, lvl4_ref, lvl8_ref, y_ref, *, layer):
    s = pl.program_id(0)
    bsz, ts, _ = x_ref.shape
    sub, g, d, base = SUB_ROWS, POOL_GROUP_DIM, SGU_HEAD_DIM, POOL_BASE
    chains = _chains(x_ref)

    @pl.when(s == 0)
    def _():
        w_in_ref[...] = w_in_f32_ref[...].astype(_BF16)
        w_out_ref[...] = w_out_f32_ref[...].astype(_BF16)
        pool_bd_ref[...] = jnp.zeros(pool_bd_ref.shape, _BF16)
        for k in range(len(POOL_WINDOWS)):
            off = (k % 2) * g
            pool_bd_ref[k // 2, off:off + g, off:off + g] = pool_w_ref[k].astype(_BF16)
        tri = (lax.broadcasted_iota(jnp.int32, (CHUNK, CHUNK), 0)
               >= lax.broadcasted_iota(jnp.int32, (CHUNK, CHUNK), 1))
        for h in range(SGU_HEADS):
            w_tri_ref[h] = jnp.where(tri, sgu_w_ref[h], 0.0).astype(_BF16)
        for b in range(bsz):
            ext_ref[b, 0:base, :] = jnp.zeros((base, POOL_WIDTH), _F32)
            lvl2_ref[b, 0:POOL_PAD, :] = jnp.zeros((POOL_PAD, POOL_WIDTH), _F32)
            lvl4_ref[b, 0:POOL_PAD, :] = jnp.zeros((POOL_PAD, POOL_WIDTH), _F32)

    xf, a, u, v, pooled, vn = {}, {}, {}, {}, {}, {}
    for ch in chains:
        b, r = ch
        xf[ch] = x_ref[b, r:r + sub, :]
        xb = xf[ch].astype(_BF16)
        a[ch] = _dot(xb, w_in_ref[:, 0:POOL_WIDTH])
        ext_ref[b, base + r:base + r + sub, :] = a[ch]
        u[ch] = _dot(xb, w_in_ref[:, POOL_WIDTH:POOL_WIDTH + SGU_WIDTH])
        v[ch] = _dot(xb, w_in_ref[:, POOL_WIDTH + SGU_WIDTH:])

    for ch in chains:
        b, r = ch
        lo = POOL_PAD if r == 0 else base + r
        hi = base + r + sub
        own = slice(base + r, hi)
        lvl2_ref[b, lo:hi, :] = ext_ref[b, lo:hi, :] + ext_ref[b, lo - 1:hi - 1, :]
        lvl4_ref[b, lo:hi, g:] = lvl2_ref[b, lo:hi, g:] + lvl2_ref[b, lo - 2:hi - 2, g:]
        lvl8_ref[b, lo:hi, 2 * g:] = lvl4_ref[b, lo:hi, 2 * g:] + lvl4_ref[b, lo - 4:hi - 4, 2 * g:]
        sums = (
            lvl2_ref[b, own, 0:g],
            lvl4_ref[b, own, g:2 * g],
            lvl8_ref[b, own, 2 * g:3 * g],
            lvl8_ref[b, own, 3 * g:4 * g] + lvl8_ref[b, base + r - 8:hi - 8, 3 * g:4 * g],
        )
        t1 = lax.broadcasted_iota(jnp.int32, (sub, g), 0) + (s * ts + r + 1)
        parts = []
        for gi, w in enumerate(POOL_WINDOWS):
            cnt = jnp.minimum(t1, w).astype(_F32)
            parts.append((sums[gi] / cnt - a[ch][:, gi * g:(gi + 1) * g]).astype(_BF16))
        pooled[ch] = parts
        vn[ch] = _layer_norm(v[ch], sgu_g_ref[...], sgu_b_ref[...]).astype(_BF16)

    for b in range(bsz):
        ext_ref[b, POOL_PAD:base, :] = a[(b, ts - sub)][sub - POOL_HALO:sub, :]

    scale = pool_scale_ref[...]
    n_chunks = sub // CHUNK
    for ch in chains:
        b, r = ch
        for pair in range(2):
            pb = jnp.concatenate(pooled[ch][2 * pair:2 * pair + 2], axis=1)
            ya = _dot(pb, pool_bd_ref[pair]) * scale[:, 2 * g * pair:2 * g * (pair + 1)]
            y_ref[b, r:r + sub, 2 * g * pair:2 * g * (pair + 1)] = ya.astype(_BF16)
        for h in range(SGU_HEADS):
            rhs = jnp.concatenate(
                [vn[ch][c * CHUNK:(c + 1) * CHUNK, h * d:(h + 1) * d] for c in range(n_chunks)],
                axis=1)
            mixed = _dot(w_tri_ref[h], rhs)
            bias_h = sgu_bias_ref[:, h * d:(h + 1) * d]
            for c in range(n_chunks):
                m = mixed[:, c * d:(c + 1) * d] + bias_h
                yb = u[ch][c * CHUNK:(c + 1) * CHUNK, h * d:(h + 1) * d] * m
                y_ref[b, r + c * CHUNK:r + (c + 1) * CHUNK,
                      POOL_WIDTH + h * d:POOL_WIDTH + (h + 1) * d] = yb.astype(_BF16)

    _project_and_norm(chains, lambda ch: y_ref[ch[0], ch[1]:ch[1] + sub, :], xf, w_out_ref,
                      _layer_row(ln1_g_ref, layer), _layer_row(ln1_b_ref, layer), o_ref)


def _odd_mixer_kernel(x_ref, w_in_ref, conv_w_ref, w_out_ref, ln1_g_ref, ln1_b_ref,
                      o_ref, z_ref, *, layer):
    s = pl.program_id(0)
    bsz, ts, _ = x_ref.shape
    sub = SUB_ROWS
    cd = w_out_ref.shape[0]
    chains = _chains(x_ref)

    @pl.when(s == 0)
    def _():
        for b in range(bsz):
            z_ref[b, 0:CONV_HALO, :] = jnp.zeros((CONV_HALO, cd), _F32)

    xf, z, gate_b, y = {}, {}, {}, {}
    for ch in chains:
        b, r = ch
        xf[ch] = x_ref[b, r:r + sub, :]
        xb = xf[ch].astype(_BF16)
        gate_c = _dot(xb, w_in_ref[:, cd:2 * cd])
        hid = _dot(xb, w_in_ref[:, 2 * cd:3 * cd])
        z[ch] = gate_c * hid
        z_ref[b, CONV_HALO + r:CONV_HALO + r + sub, :] = z[ch]
        gate_b[ch] = _dot(xb, w_in_ref[:, 0:cd])
    cw = conv_w_ref[...]
    for ch in chains:
        b, r = ch
        conv = cw[CONV_WIDTH - 1:CONV_WIDTH, :] * z[ch]
        for k in range(CONV_WIDTH - 1):
            off = CONV_HALO + r - (CONV_WIDTH - 1) + k
            conv = conv + cw[k:k + 1, :] * z_ref[b, off:off + sub, :]
        y[ch] = (gate_b[ch] * conv).astype(_BF16)
    for b in range(bsz):
        z_ref[b, 0:CONV_HALO, :] = z[(b, ts - sub)][sub - CONV_HALO:sub, :]
    _project_and_norm(chains, lambda ch: y[ch], xf, w_out_ref,
                      _layer_row(ln1_g_ref, layer), _layer_row(ln1_b_ref, layer), o_ref)


def _ffn_kernel(x_ref, p_ref, w1_ref, w2_ref, ln2_g_ref, ln2_b_ref, wg_ref, wp_ref, o_ref,
                *, layer):
    sub = SUB_ROWS
    gain, bias = _layer_row(ln2_g_ref, layer), _layer_row(ln2_b_ref, layer)
    d_ff = w1_ref.shape[1]
    ff_chunk = w1_ref.shape[0]
    chains = _chains(x_ref)

    def mlp(ch):
        b, r = ch
        xf = x_ref[b, r:r + sub, :]
        xb = xf.astype(_BF16)
        hidden = []
        for c in range(d_ff // ff_chunk):
            h = _dot(xb, w1_ref[:, c * ff_chunk:(c + 1) * ff_chunk])
            h = jnp.maximum(h, 0.0)
            hidden.append((h * h).astype(_BF16))
        acc = _dot(jnp.concatenate(hidden, axis=1), w2_ref[...])
        return _layer_norm(DEEPNORM_ALPHA * xf + acc, gain, bias)

    def embed_gate(ch, x2):
        b, r = ch
        gate = jax.nn.sigmoid(_dot(x2.astype(_BF16), wg_ref[...]))
        pp = _dot(p_ref[b, r:r + sub, :].astype(_BF16), wp_ref[...])
        o_ref[b, r:r + sub, :] = x2 + gate * pp

    pending = None
    for ch in chains:
        x2 = mlp(ch)
        if pending is not None:
            embed_gate(*pending)
        pending = (ch, x2)
    embed_gate(*pending)


def _with_weight_casts(body, n_in, n_cast):
    def kern(*refs):
        ins, srcs = refs[:n_in], refs[n_in:n_in + n_cast]
        out = refs[n_in + n_cast]
        dsts = refs[n_in + n_cast + 1:n_in + 2 * n_cast + 1]
        scratch = refs[n_in + 2 * n_cast + 1:]
        body(*ins, out, *scratch)
        for src, dst in zip(srcs, dsts):
            dst[...] = src[...].astype(_BF16)
    return kern


def _resident(arr, index=None):
    if index is None:
        zeros = (0,) * arr.ndim
        return pl.BlockSpec(arr.shape, lambda s: zeros, pipeline_mode=pl.Buffered(1))
    zeros = (0,) * (arr.ndim - 1)
    return pl.BlockSpec((None,) + arr.shape[1:], lambda s: (index,) + zeros,
                        pipeline_mode=pl.Buffered(1))


def _call(name, body, x, streamed, resident, scratch, casts):
    bsz, seq, d = x.shape
    ts = SEQ_TILE
    steps = seq // ts
    act = pl.BlockSpec((bsz, ts, d), lambda s: (0, s, 0))
    cast_in, cast_out, cast_shapes = [], [], []
    for w, idx in casts:
        _, rows, cols = w.shape
        slab = max(rows // steps, 2 * SUBLANES)
        last = rows // slab - 1
        assert slab * (last + 1) == rows and slab % (2 * SUBLANES) == 0 and last < steps
        cast_in.append(pl.BlockSpec((None, slab, cols),
                                    lambda s, idx=idx, last=last: (idx, jnp.minimum(s, last), 0)))
        cast_out.append(pl.BlockSpec((slab, cols), lambda s, last=last: (jnp.minimum(s, last), 0)))
        cast_shapes.append(jax.ShapeDtypeStruct((rows, cols), _BF16))
    n_in = 1 + len(streamed) + len(resident)
    outs = pl.pallas_call(
        _with_weight_casts(body, n_in, len(casts)),
        grid=(steps,),
        in_specs=([act] + [spec for _, spec in streamed]
                  + [_resident(a, idx) for a, idx in resident] + cast_in),
        out_specs=[act] + cast_out,
        out_shape=[jax.ShapeDtypeStruct(x.shape, x.dtype)] + cast_shapes,
        scratch_shapes=scratch,
        compiler_params=pltpu.CompilerParams(dimension_semantics=("arbitrary",),
                                             vmem_limit_bytes=VMEM_LIMIT_BYTES),
        name=name,
    )(x, *[a for a, _ in streamed], *[a for a, _ in resident], *[w for w, _ in casts])
    return outs[0], outs[1:]


def kernel(x, p, ev_w_in, ev_pool_w, ev_pool_scale, ev_sgu_ln_g, ev_sgu_ln_b, ev_sgu_w, ev_sgu_b,
           ev_w_out, od_w_in, od_conv_w, od_w_out, ln1_g, ln1_b, ffn_w1, ffn_w2, ln2_g, ln2_b,
           ple_gate_w, ple_w):
    assert x.shape[1] % SEQ_TILE == 0 and SEQ_TILE % SUB_ROWS == 0 and SUB_ROWS % CHUNK == 0
    assert ln1_g.shape[0] == DEPTH and ev_w_in.shape[0] == 1 and od_w_in.shape[0] == 1
    bsz, _, d = x.shape
    p_spec = lambda i: pl.BlockSpec((None, bsz, SEQ_TILE, p.shape[-1]), lambda s: (i, 0, s, 0))
    ffn_casts = lambda i: [(ffn_w1, i), (ffn_w2, i), (ple_gate_w, i), (ple_w, i)]

    sgu_bias = jnp.repeat(ev_sgu_b[0].T, SGU_HEAD_DIM, axis=1)
    resident = [(ev_w_in, 0), (ev_pool_w, 0), (ev_pool_scale, None), (ev_sgu_ln_g, None),
                (ev_sgu_ln_b, None), (ev_sgu_w, 0), (sgu_bias, None), (ev_w_out, 0),
                (ln1_g, None), (ln1_b, None)]
    pool_buf = pltpu.VMEM((bsz, POOL_BASE + SEQ_TILE, POOL_WIDTH), _F32)
    scratch = [pltpu.VMEM(ev_w_in.shape[1:], _BF16), pltpu.VMEM(ev_w_out.shape[1:], _BF16),
               pltpu.VMEM((2, 2 * POOL_GROUP_DIM, 2 * POOL_GROUP_DIM), _BF16),
               pltpu.VMEM((SGU_HEADS, CHUNK, CHUNK), _BF16),
               pool_buf, pool_buf, pool_buf, pool_buf, pltpu.VMEM((bsz, SEQ_TILE, d), _BF16)]
    x, (w1, w2, wg, wp) = _call("even_mixer", functools.partial(_even_mixer_kernel, layer=0),
                                x, [], resident, scratch, ffn_casts(0))
    resident = [(w1, None), (w2, None), (ln2_g, None), (ln2_b, None), (wg, None), (wp, None)]
    x, (w_in, w_out) = _call("ffn_0", functools.partial(_ffn_kernel, layer=0), x,
                             [(p, p_spec(0))], resident, [], [(od_w_in, 0), (od_w_out, 0)])

    resident = [(w_in, None), (od_conv_w, 0), (w_out, None), (ln1_g, None), (ln1_b, None)]
    scratch = [pltpu.VMEM((bsz, CONV_HALO + SEQ_TILE, w_out.shape[0]), _F32)]
    x, (w1, w2, wg, wp) = _call("odd_mixer", functools.partial(_odd_mixer_kernel, layer=1),
                                x, [], resident, scratch, ffn_casts(1))
    resident = [(w1, None), (w2, None), (ln2_g, None), (ln2_b, None), (wg, None), (wp, None)]
    x, _ = _call("ffn_1", functools.partial(_ffn_kernel, layer=1), x, [(p, p_spec(1))],
                 resident, [], [])
    return x
```

```python
import functools

import jax
import jax.numpy as jnp
from jax import lax
from jax.experimental import pallas as pl
from jax.experimental.pallas import tpu as pltpu

POOL_WINDOWS = (2, 4, 8, 16)
POOL_GROUP_DIM = 128
POOL_WIDTH = POOL_GROUP_DIM * len(POOL_WINDOWS)
SGU_HEADS = 4
SGU_HEAD_DIM = 128
SGU_WIDTH = SGU_HEADS * SGU_HEAD_DIM
CHUNK = 128
CONV_WIDTH = 3
LN_EPS = 1e-5
DEPTH = 2
DEEPNORM_ALPHA = (2.0 * DEPTH) ** 0.25

SUBLANES = 8
POOL_HALO = max(POOL_WINDOWS)
POOL_PAD = SUBLANES
POOL_BASE = POOL_PAD + POOL_HALO
CONV_HALO = SUBLANES

SEQ_TILE = 512
SUB_ROWS = 256
LHS_SLOTS = 2
VMEM_LIMIT_BYTES = 56 * 1024 * 1024

_BF16 = jnp.bfloat16
_F32 = jnp.float32


def _dot(a, b):
    return jnp.dot(a, b, preferred_element_type=_F32)


def _layer_norm(x, g, b):
    mu = jnp.mean(x, axis=-1, keepdims=True)
    xc = x - mu
    var = jnp.mean(xc * xc, axis=-1, keepdims=True)
    return xc * lax.rsqrt(var + LN_EPS) * g + b


def _chains(x_ref):
    bsz, ts, _ = x_ref.shape
    return [(b, r) for b in range(bsz) for r in range(0, ts, SUB_ROWS)]


def _layer_row(ref, layer):
    return ref[layer:layer + 1, :]


def _even_mixer_kernel(x_ref, w_in_f32_ref, pool_w_ref, pool_scale_ref, sgu_g_ref, sgu_b_ref,
                       sgu_w_ref, sgu_pos_b_ref, y_ref, w_in_ref, pool_bd_ref, w_tri_ref,
                       sgu_bias_ref, ext_ref, lvl2_ref, lvl4_ref, lvl8_ref):
    s = pl.program_id(0)
    bsz, ts, _ = x_ref.shape
    sub, g, d, base = SUB_ROWS, POOL_GROUP_DIM, SGU_HEAD_DIM, POOL_BASE
    chains = _chains(x_ref)

    @pl.when(s == 0)
    def _():
        w_in_ref[...] = w_in_f32_ref[...].astype(_BF16)
        pool_bd_ref[...] = jnp.zeros(pool_bd_ref.shape, _BF16)
        for k in range(len(POOL_WINDOWS)):
            off = (k % 2) * g
            pool_bd_ref[k // 2, off:off + g, off:off + g] = pool_w_ref[k].astype(_BF16)
        tri = (lax.broadcasted_iota(jnp.int32, (CHUNK, CHUNK), 0)
               >= lax.broadcasted_iota(jnp.int32, (CHUNK, CHUNK), 1))
        for h in range(SGU_HEADS):
            w_tri_ref[h] = jnp.where(tri, sgu_w_ref[h], 0.0).astype(_BF16)
            pos_b = jnp.broadcast_to(sgu_pos_b_ref[h:h + 1, :], (SGU_HEAD_DIM, CHUNK))
            sgu_bias_ref[:, h * d:(h + 1) * d] = pos_b.T
        for b in range(bsz):
            ext_ref[b, 0:base, :] = jnp.zeros((base, POOL_WIDTH), _F32)
            lvl2_ref[b, 0:POOL_PAD, :] = jnp.zeros((POOL_PAD, POOL_WIDTH), _F32)
            lvl4_ref[b, 0:POOL_PAD, :] = jnp.zeros((POOL_PAD, POOL_WIDTH), _F32)

    a, u, v, pooled, vn = {}, {}, {}, {}, {}
    for ch in chains:
        b, r = ch
        xb = x_ref[b, r:r + sub, :].astype(_BF16)
        a[ch] = _dot(xb, w_in_ref[:, 0:POOL_WIDTH])
        ext_ref[b, base + r:base + r + sub, :] = a[ch]
        u[ch] = _dot(xb, w_in_ref[:, POOL_WIDTH:POOL_WIDTH + SGU_WIDTH])
        v[ch] = _dot(xb, w_in_ref[:, POOL_WIDTH + SGU_WIDTH:])

    for ch in chains:
        b, r = ch
        lo = POOL_PAD if r == 0 else base + r
        hi = base + r + sub
        own = slice(base + r, hi)
        lvl2_ref[b, lo:hi, :] = ext_ref[b, lo:hi, :] + ext_ref[b, lo - 1:hi - 1, :]
        lvl4_ref[b, lo:hi, g:] = lvl2_ref[b, lo:hi, g:] + lvl2_ref[b, lo - 2:hi - 2, g:]
        lvl8_ref[b, lo:hi, 2 * g:] = lvl4_ref[b, lo:hi, 2 * g:] + lvl4_ref[b, lo - 4:hi - 4, 2 * g:]
        sums = (
            lvl2_ref[b, own, 0:g],
            lvl4_ref[b, own, g:2 * g],
            lvl8_ref[b, own, 2 * g:3 * g],
            lvl8_ref[b, own, 3 * g:4 * g] + lvl8_ref[b, base + r - 8:hi - 8, 3 * g:4 * g],
        )
        t1 = lax.broadcasted_iota(jnp.int32, (sub, g), 0) + (s * ts + r + 1)
        parts = []
        for gi, w in enumerate(POOL_WINDOWS):
            cnt = jnp.minimum(t1, w).astype(_F32)
            parts.append((sums[gi] / cnt - a[ch][:, gi * g:(gi + 1) * g]).astype(_BF16))
        pooled[ch] = parts
        vn[ch] = _layer_norm(v[ch], sgu_g_ref[...], sgu_b_ref[...]).astype(_BF16)

    for b in range(bsz):
        ext_ref[b, POOL_PAD:base, :] = a[(b, ts - sub)][sub - POOL_HALO:sub, :]

    scale = pool_scale_ref[...]
    n_chunks = sub // CHUNK
    for ch in chains:
        b, r = ch
        for pair in range(2):
            pb = jnp.concatenate(pooled[ch][2 * pair:2 * pair + 2], axis=1)
            ya = _dot(pb, pool_bd_ref[pair]) * scale[:, 2 * g * pair:2 * g * (pair + 1)]
            y_ref[b, r:r + sub, 2 * g * pair:2 * g * (pair + 1)] = ya.astype(_BF16)
        for h in range(SGU_HEADS):
            rhs = jnp.concatenate(
                [vn[ch][c * CHUNK:(c + 1) * CHUNK, h * d:(h + 1) * d] for c in range(n_chunks)],
                axis=1)
            mixed = _dot(w_tri_ref[h], rhs)
            bias_h = sgu_bias_ref[:, h * d:(h + 1) * d]
            for c in range(n_chunks):
                m = mixed[:, c * d:(c + 1) * d] + bias_h
                yb = u[ch][c * CHUNK:(c + 1) * CHUNK, h * d:(h + 1) * d] * m
                y_ref[b, r + c * CHUNK:r + (c + 1) * CHUNK,
                      POOL_WIDTH + h * d:POOL_WIDTH + (h + 1) * d] = yb.astype(_BF16)


def _odd_mixer_kernel(x_ref, w_in_ref, conv_w_ref, y_ref, z_ref):
    s = pl.program_id(0)
    bsz, ts, _ = x_ref.shape
    sub = SUB_ROWS
    cd = conv_w_ref.shape[-1]
    chains = _chains(x_ref)

    @pl.when(s == 0)
    def _():
        for b in range(bsz):
            z_ref[b, 0:CONV_HALO, :] = jnp.zeros((CONV_HALO, cd), _F32)

    z, gate_b = {}, {}
    for ch in chains:
        b, r = ch
        xb = x_ref[b, r:r + sub, :].astype(_BF16)
        gate_c = _dot(xb, w_in_ref[:, cd:2 * cd])
        hid = _dot(xb, w_in_ref[:, 2 * cd:3 * cd])
        z[ch] = gate_c * hid
        z_ref[b, CONV_HALO + r:CONV_HALO + r + sub, :] = z[ch]
        gate_b[ch] = _dot(xb, w_in_ref[:, 0:cd])
    cw = conv_w_ref[...]
    for ch in chains:
        b, r = ch
        conv = cw[CONV_WIDTH - 1:CONV_WIDTH, :] * z[ch]
        for k in range(CONV_WIDTH - 1):
            off = CONV_HALO + r - (CONV_WIDTH - 1) + k
            conv = conv + cw[k:k + 1, :] * z_ref[b, off:off + sub, :]
        y_ref[b, r:r + sub, :] = (gate_b[ch] * conv).astype(_BF16)
    for b in range(bsz):
        z_ref[b, 0:CONV_HALO, :] = z[(b, ts - sub)][sub - CONV_HALO:sub, :]


def _channel_kernel(x_ref, y_ref, p_ref, w_out_ref, ln1_g_ref, ln1_b_ref, w1_ref, w2_ref,
                    ln2_g_ref, ln2_b_ref, wg_ref, wp_ref, o_ref, xb_ref, hid_ref, x2b_ref,
                    *, layer):
    sub = SUB_ROWS
    gain1, bias1 = _layer_row(ln1_g_ref, layer), _layer_row(ln1_b_ref, layer)
    gain2, bias2 = _layer_row(ln2_g_ref, layer), _layer_row(ln2_b_ref, layer)
    d_ff = w1_ref.shape[1]
    ff_chunk = w1_ref.shape[0]
    chains = _chains(x_ref)

    def out_proj(ch):
        b, r = ch
        return _dot(y_ref[b, r:r + sub, :], w_out_ref[...])

    def mlp(ch, slot, mix):
        b, r = ch
        x1 = _layer_norm(DEEPNORM_ALPHA * x_ref[b, r:r + sub, :] + mix, gain1, bias1)
        xb_ref[slot] = x1.astype(_BF16)
        for c in range(d_ff // ff_chunk):
            h = _dot(xb_ref[slot], w1_ref[:, c * ff_chunk:(c + 1) * ff_chunk])
            h = jnp.maximum(h, 0.0)
            hid_ref[slot, :, c * ff_chunk:(c + 1) * ff_chunk] = (h * h).astype(_BF16)
        acc = _dot(hid_ref[slot], w2_ref[...])
        return _layer_norm(DEEPNORM_ALPHA * x1 + acc, gain2, bias2)

    def embed_gate(ch, slot, x2):
        b, r = ch
        x2b_ref[slot] = x2.astype(_BF16)
        gate = jax.nn.sigmoid(_dot(x2b_ref[slot], wg_ref[...]))
        pp = _dot(p_ref[b, r:r + sub, :].astype(_BF16), wp_ref[...])
        o_ref[b, r:r + sub, :] = x2 + gate * pp

    mix = out_proj(chains[0])
    pending = None
    for i, ch in enumerate(chains):
        next_mix = out_proj(chains[i + 1]) if i + 1 < len(chains) else None
        x2 = mlp(ch, i % LHS_SLOTS, mix)
        if pending is not None:
            embed_gate(*pending)
        pending = (ch, i % LHS_SLOTS, x2)
        mix = next_mix
    embed_gate(*pending)


def _with_weight_casts(body, n_in, n_cast):
    def kern(*refs):
        ins, srcs = refs[:n_in], refs[n_in:n_in + n_cast]
        out = refs[n_in + n_cast]
        dsts = refs[n_in + n_cast + 1:n_in + 2 * n_cast + 1]
        scratch = refs[n_in + 2 * n_cast + 1:]
        body(*ins, out, *scratch)
        for src, dst in zip(srcs, dsts):
            dst[...] = src[...].astype(_BF16)
    return kern


def _resident(arr, index=None):
    if index is None:
        zeros = (0,) * arr.ndim
        return pl.BlockSpec(arr.shape, lambda s: zeros, pipeline_mode=pl.Buffered(1))
    zeros = (0,) * (arr.ndim - 1)
    return pl.BlockSpec((None,) + arr.shape[1:], lambda s: (index,) + zeros,
                        pipeline_mode=pl.Buffered(1))


def _call(name, body, streamed, resident, out_dtype, scratch, casts):
    bsz, seq, d = streamed[0][0].shape
    ts = SEQ_TILE
    steps = seq // ts
    act = pl.BlockSpec((bsz, ts, d), lambda s: (0, s, 0))
    cast_in, cast_out, cast_shapes = [], [], []
    for w, idx in casts:
        _, rows, cols = w.shape
        slab = max(rows // steps, 2 * SUBLANES)
        last = rows // slab - 1
        assert slab * (last + 1) == rows and slab % (2 * SUBLANES) == 0 and last < steps
        cast_in.append(pl.BlockSpec((None, slab, cols),
                                    lambda s, idx=idx, last=last: (idx, jnp.minimum(s, last), 0)))
        cast_out.append(pl.BlockSpec((slab, cols), lambda s, last=last: (jnp.minimum(s, last), 0)))
        cast_shapes.append(jax.ShapeDtypeStruct((rows, cols), _BF16))
    n_in = len(streamed) + len(resident)
    outs = pl.pallas_call(
        _with_weight_casts(body, n_in, len(casts)),
        grid=(steps,),
        in_specs=([act if spec is None else spec for _, spec in streamed]
                  + [_resident(a, idx) for a, idx in resident] + cast_in),
        out_specs=[act] + cast_out,
        out_shape=[jax.ShapeDtypeStruct((bsz, seq, d), out_dtype)] + cast_shapes,
        scratch_shapes=scratch,
        compiler_params=pltpu.CompilerParams(dimension_semantics=("arbitrary",),
                                             vmem_limit_bytes=VMEM_LIMIT_BYTES),
        name=name,
    )(*[a for a, _ in streamed], *[a for a, _ in resident], *[w for w, _ in casts])
    return outs[0], outs[1:]


def kernel(x, p, ev_w_in, ev_pool_w, ev_pool_scale, ev_sgu_ln_g, ev_sgu_ln_b, ev_sgu_w, ev_sgu_b,
           ev_w_out, od_w_in, od_conv_w, od_w_out, ln1_g, ln1_b, ffn_w1, ffn_w2, ln2_g, ln2_b,
           ple_gate_w, ple_w):
    assert x.shape[1] % SEQ_TILE == 0 and SEQ_TILE % SUB_ROWS == 0 and SUB_ROWS % CHUNK == 0
    assert ln1_g.shape[0] == DEPTH and ev_w_in.shape[0] == 1 and od_w_in.shape[0] == 1
    bsz, _, d = x.shape
    p_spec = lambda i: pl.BlockSpec((None, bsz, SEQ_TILE, p.shape[-1]), lambda s: (i, 0, s, 0))
    channel_casts = lambda w_out, i: [(w_out, 0), (ffn_w1, i), (ffn_w2, i), (ple_gate_w, i),
                                      (ple_w, i)]
    channel_scratch = [pltpu.VMEM((LHS_SLOTS, SUB_ROWS, d), _BF16),
                       pltpu.VMEM((LHS_SLOTS, SUB_ROWS, ffn_w1.shape[-1]), _BF16),
                       pltpu.VMEM((LHS_SLOTS, SUB_ROWS, d), _BF16)]

    def channel(i, x, y, weights, casts):
        w_out, w1, w2, wg, wp = weights
        resident = [(w_out, None), (ln1_g, None), (ln1_b, None), (w1, None), (w2, None),
                    (ln2_g, None), (ln2_b, None), (wg, None), (wp, None)]
        return _call(f"channel_{i}", functools.partial(_channel_kernel, layer=i),
                     [(x, None), (y, None), (p, p_spec(i))], resident, x.dtype,
                     channel_scratch, casts)

    resident = [(ev_w_in, 0), (ev_pool_w, 0), (ev_pool_scale, None), (ev_sgu_ln_g, None),
                (ev_sgu_ln_b, None), (ev_sgu_w, 0), (ev_sgu_b, 0)]
    pool_buf = pltpu.VMEM((bsz, POOL_BASE + SEQ_TILE, POOL_WIDTH), _F32)
    scratch = [pltpu.VMEM(ev_w_in.shape[1:], _BF16),
               pltpu.VMEM((2, 2 * POOL_GROUP_DIM, 2 * POOL_GROUP_DIM), _BF16),
               pltpu.VMEM((SGU_HEADS, CHUNK, CHUNK), _BF16),
               pltpu.VMEM((CHUNK, SGU_WIDTH), _F32),
               pool_buf, pool_buf, pool_buf, pool_buf]
    y, weights = _call("even_mixer", _even_mixer_kernel, [(x, None)], resident, _BF16, scratch,
                       channel_casts(ev_w_out, 0))
    x, (w_in,) = channel(0, x, y, weights, [(od_w_in, 0)])

    scratch = [pltpu.VMEM((bsz, CONV_HALO + SEQ_TILE, od_conv_w.shape[-1]), _F32)]
    y, weights = _call("odd_mixer", _odd_mixer_kernel, [(x, None)],
                       [(w_in, None), (od_conv_w, 0)], _BF16, scratch,
                       channel_casts(od_w_out, 1))
    x, _ = channel(1, x, y, weights, [])
    return x
```

```python
import functools

import jax
import jax.numpy as jnp
from jax import lax
from jax.experimental import pallas as pl
from jax.experimental.pallas import tpu as pltpu

POOL_WINDOWS = (2, 4, 8, 16)
POOL_GROUP_DIM = 128
POOL_WIDTH = POOL_GROUP_DIM * len(POOL_WINDOWS)
SGU_HEADS = 4
SGU_HEAD_DIM = 128
SGU_WIDTH = SGU_HEADS * SGU_HEAD_DIM
CHUNK = 128
CONV_WIDTH = 3
LN_EPS = 1e-5
DEPTH = 2
DEEPNORM_ALPHA = (2.0 * DEPTH) ** 0.25

SUBLANES = 8
POOL_HALO = max(POOL_WINDOWS)
POOL_PAD = SUBLANES
POOL_BASE = POOL_PAD + POOL_HALO
CONV_HALO = SUBLANES

SEQ_TILE = 512
SUB_ROWS = 256
LHS_SLOTS = 2
VMEM_LIMIT_BYTES = 60 * 1024 * 1024

_BF16 = jnp.bfloat16
_F32 = jnp.float32


def _dot(a, b):
    return jnp.dot(a, b, preferred_element_type=_F32)


def _layer_norm(x, g, b):
    mu = jnp.mean(x, axis=-1, keepdims=True)
    xc = x - mu
    var = jnp.mean(xc * xc, axis=-1, keepdims=True)
    return xc * lax.rsqrt(var + LN_EPS) * g + b


def _chains(x_ref):
    bsz, ts, _ = x_ref.shape
    return [(b, r) for b in range(bsz) for r in range(0, ts, SUB_ROWS)]


def _layer_row(ref, layer):
    return ref[layer:layer + 1, :]


def _even_mixer_kernel(x_ref, w_in_f32_ref, pool_w_ref, pool_scale_ref, sgu_g_ref, sgu_b_ref,
                       sgu_w_ref, sgu_pos_b_ref, y_ref, w_in_ref, pool_bd_ref, w_tri_ref,
                       sgu_bias_ref, ext_ref, lvl2_ref, lvl4_ref, lvl8_ref):
    s = pl.program_id(0)
    bsz, ts, _ = x_ref.shape
    sub, g, d, base = SUB_ROWS, POOL_GROUP_DIM, SGU_HEAD_DIM, POOL_BASE
    chains = _chains(x_ref)

    @pl.when(s == 0)
    def _():
        w_in_ref[...] = w_in_f32_ref[...].astype(_BF16)
        pool_bd_ref[...] = jnp.zeros(pool_bd_ref.shape, _BF16)
        for k in range(len(POOL_WINDOWS)):
            off = (k % 2) * g
            pool_bd_ref[k // 2, off:off + g, off:off + g] = pool_w_ref[k].astype(_BF16)
        tri = (lax.broadcasted_iota(jnp.int32, (CHUNK, CHUNK), 0)
               >= lax.broadcasted_iota(jnp.int32, (CHUNK, CHUNK), 1))
        for h in range(SGU_HEADS):
            w_tri_ref[h] = jnp.where(tri, sgu_w_ref[h], 0.0).astype(_BF16)
            pos_b = jnp.broadcast_to(sgu_pos_b_ref[h:h + 1, :], (SGU_HEAD_DIM, CHUNK))
            sgu_bias_ref[:, h * d:(h + 1) * d] = pos_b.T
        for b in range(bsz):
            ext_ref[b, 0:base, :] = jnp.zeros((base, POOL_WIDTH), _F32)
            lvl2_ref[b, 0:POOL_PAD, :] = jnp.zeros((POOL_PAD, POOL_WIDTH), _F32)
            lvl4_ref[b, 0:POOL_PAD, :] = jnp.zeros((POOL_PAD, POOL_WIDTH), _F32)

    a, u, v, pooled, vn = {}, {}, {}, {}, {}
    for ch in chains:
        b, r = ch
        xb = x_ref[b, r:r + sub, :].astype(_BF16)
        a[ch] = _dot(xb, w_in_ref[:, 0:POOL_WIDTH])
        ext_ref[b, base + r:base + r + sub, :] = a[ch]
        u[ch] = _dot(xb, w_in_ref[:, POOL_WIDTH:POOL_WIDTH + SGU_WIDTH])
        v[ch] = _dot(xb, w_in_ref[:, POOL_WIDTH + SGU_WIDTH:])

    for ch in chains:
        b, r = ch
        lo = POOL_PAD if r == 0 else base + r
        hi = base + r + sub
        own = slice(base + r, hi)
        lvl2_ref[b, lo:hi, :] = ext_ref[b, lo:hi, :] + ext_ref[b, lo - 1:hi - 1, :]
        lvl4_ref[b, lo:hi, g:] = lvl2_ref[b, lo:hi, g:] + lvl2_ref[b, lo - 2:hi - 2, g:]
        lvl8_ref[b, lo:hi, 2 * g:] = lvl4_ref[b, lo:hi, 2 * g:] + lvl4_ref[b, lo - 4:hi - 4, 2 * g:]
        sums = (
            lvl2_ref[b, own, 0:g],
            lvl4_ref[b, own, g:2 * g],
            lvl8_ref[b, own, 2 * g:3 * g],
            lvl8_ref[b, own, 3 * g:4 * g] + lvl8_ref[b, base + r - 8:hi - 8, 3 * g:4 * g],
        )
        t1 = lax.broadcasted_iota(jnp.int32, (sub, g), 0) + (s * ts + r + 1)
        parts = []
        for gi, w in enumerate(POOL_WINDOWS):
            cnt = jnp.minimum(t1, w).astype(_F32)
            parts.append((sums[gi] / cnt - a[ch][:, gi * g:(gi + 1) * g]).astype(_BF16))
        pooled[ch] = parts
        vn[ch] = _layer_norm(v[ch], sgu_g_ref[...], sgu_b_ref[...]).astype(_BF16)

    for b in range(bsz):
        ext_ref[b, POOL_PAD:base, :] = a[(b, ts - sub)][sub - POOL_HALO:sub, :]

    scale = pool_scale_ref[...]
    n_chunks = sub // CHUNK
    for ch in chains:
        b, r = ch
        for pair in range(2):
            pb = jnp.concatenate(pooled[ch][2 * pair:2 * pair + 2], axis=1)
            ya = _dot(pb, pool_bd_ref[pair]) * scale[:, 2 * g * pair:2 * g * (pair + 1)]
            y_ref[b, r:r + sub, 2 * g * pair:2 * g * (pair + 1)] = ya.astype(_BF16)
        for h in range(SGU_HEADS):
            rhs = jnp.concatenate(
                [vn[ch][c * CHUNK:(c + 1) * CHUNK, h * d:(h + 1) * d] for c in range(n_chunks)],
                axis=1)
            mixed = _dot(w_tri_ref[h], rhs)
            bias_h = sgu_bias_ref[:, h * d:(h + 1) * d]
            for c in range(n_chunks):
                m = mixed[:, c * d:(c + 1) * d] + bias_h
                yb = u[ch][c * CHUNK:(c + 1) * CHUNK, h * d:(h + 1) * d] * m
                y_ref[b, r + c * CHUNK:r + (c + 1) * CHUNK,
                      POOL_WIDTH + h * d:POOL_WIDTH + (h + 1) * d] = yb.astype(_BF16)


def _odd_mixer_kernel(x_ref, w_in_ref, conv_w_ref, y_ref, z_ref):
    s = pl.program_id(0)
    bsz, ts, _ = x_ref.shape
    sub = SUB_ROWS
    cd = conv_w_ref.shape[-1]
    chains = _chains(x_ref)

    @pl.when(s == 0)
    def _():
        for b in range(bsz):
            z_ref[b, 0:CONV_HALO, :] = jnp.zeros((CONV_HALO, cd), _F32)

    z, gate_b = {}, {}
    for ch in chains:
        b, r = ch
        xb = x_ref[b, r:r + sub, :].astype(_BF16)
        gate_c = _dot(xb, w_in_ref[:, cd:2 * cd])
        hid = _dot(xb, w_in_ref[:, 2 * cd:3 * cd])
        z[ch] = gate_c * hid
        z_ref[b, CONV_HALO + r:CONV_HALO + r + sub, :] = z[ch]
        gate_b[ch] = _dot(xb, w_in_ref[:, 0:cd])
    cw = conv_w_ref[...]
    for ch in chains:
        b, r = ch
        conv = cw[CONV_WIDTH - 1:CONV_WIDTH, :] * z[ch]
        for k in range(CONV_WIDTH - 1):
            off = CONV_HALO + r - (CONV_WIDTH - 1) + k
            conv = conv + cw[k:k + 1, :] * z_ref[b, off:off + sub, :]
        y_ref[b, r:r + sub, :] = (gate_b[ch] * conv).astype(_BF16)
    for b in range(bsz):
        z_ref[b, 0:CONV_HALO, :] = z[(b, ts - sub)][sub - CONV_HALO:sub, :]


def _channel_kernel(x_ref, y_ref, p_ref, w_out_ref, ln1_g_ref, ln1_b_ref, w1_ref, w2_ref,
                    ln2_g_ref, ln2_b_ref, wg_ref, wp_ref, o_ref, xb_ref, hid_ref, x2b_ref,
                    *, layer):
    sub = SUB_ROWS
    gain1, bias1 = _layer_row(ln1_g_ref, layer), _layer_row(ln1_b_ref, layer)
    gain2, bias2 = _layer_row(ln2_g_ref, layer), _layer_row(ln2_b_ref, layer)
    d_ff = w1_ref.shape[1]
    ff_chunk = w1_ref.shape[0]
    chains = _chains(x_ref)

    def out_proj(ch):
        b, r = ch
        return _dot(y_ref[b, r:r + sub, :], w_out_ref[...])

    def mlp(ch, slot, mix):
        b, r = ch
        x1 = _layer_norm(DEEPNORM_ALPHA * x_ref[b, r:r + sub, :] + mix, gain1, bias1)
        xb_ref[slot] = x1.astype(_BF16)
        for c in range(d_ff // ff_chunk):
            h = _dot(xb_ref[slot], w1_ref[:, c * ff_chunk:(c + 1) * ff_chunk])
            h = jnp.maximum(h, 0.0)
            hid_ref[slot, :, c * ff_chunk:(c + 1) * ff_chunk] = (h * h).astype(_BF16)
        acc = _dot(hid_ref[slot], w2_ref[...])
        return _layer_norm(DEEPNORM_ALPHA * x1 + acc, gain2, bias2)

    def embed_gate(ch, slot, x2):
        b, r = ch
        x2b_ref[slot] = x2.astype(_BF16)
        gate = jax.nn.sigmoid(_dot(x2b_ref[slot], wg_ref[...]))
        pp = _dot(p_ref[b, r:r + sub, :].astype(_BF16), wp_ref[...])
        o_ref[b, r:r + sub, :] = x2 + gate * pp

    mix = out_proj(chains[0])
    pending = None
    for i, ch in enumerate(chains):
        next_mix = out_proj(chains[i + 1]) if i + 1 < len(chains) else None
        x2 = mlp(ch, i % LHS_SLOTS, mix)
        if pending is not None:
            embed_gate(*pending)
        pending = (ch, i % LHS_SLOTS, x2)
        mix = next_mix
    embed_gate(*pending)


def _with_weight_casts(body, n_in, n_cast):
    def kern(*refs):
        ins, srcs = refs[:n_in], refs[n_in:n_in + n_cast]
        out = refs[n_in + n_cast]
        dsts = refs[n_in + n_cast + 1:n_in + 2 * n_cast + 1]
        scratch = refs[n_in + 2 * n_cast + 1:]
        body(*ins, out, *scratch)
        for src, dst in zip(srcs, dsts):
            dst[...] = src[...].astype(_BF16)
    return kern


def _resident(arr, index=None):
    if index is None:
        zeros = (0,) * arr.ndim
        return pl.BlockSpec(arr.shape, lambda s: zeros, pipeline_mode=pl.Buffered(1))
    zeros = (0,) * (arr.ndim - 1)
    return pl.BlockSpec((None,) + arr.shape[1:], lambda s: (index,) + zeros,
                        pipeline_mode=pl.Buffered(1))


def _call(name, body, streamed, resident, out_dtype, scratch, casts):
    bsz, seq, d = streamed[0][0].shape
    ts = SEQ_TILE
    steps = seq // ts
    act = pl.BlockSpec((bsz, ts, d), lambda s: (0, s, 0))
    cast_in, cast_out, cast_shapes = [], [], []
    for w, idx in casts:
        _, rows, cols = w.shape
        slab = max(rows // steps, 2 * SUBLANES)
        last = rows // slab - 1
        assert slab * (last + 1) == rows and slab % (2 * SUBLANES) == 0 and last < steps
        cast_in.append(pl.BlockSpec((None, slab, cols),
                                    lambda s, idx=idx, last=last: (idx, jnp.minimum(s, last), 0)))
        cast_out.append(pl.BlockSpec((slab, cols), lambda s, last=last: (jnp.minimum(s, last), 0)))
        cast_shapes.append(jax.ShapeDtypeStruct((rows, cols), _BF16))
    n_in = len(streamed) + len(resident)
    outs = pl.pallas_call(
        _with_weight_casts(body, n_in, len(casts)),
        grid=(steps,),
        in_specs=([act if spec is None else spec for _, spec in streamed]
                  + [_resident(a, idx) for a, idx in resident] + cast_in),
        out_specs=[act] + cast_out,
        out_shape=[jax.ShapeDtypeStruct((bsz, seq, d), out_dtype)] + cast_shapes,
        scratch_shapes=scratch,
        compiler_params=pltpu.CompilerParams(dimension_semantics=("arbitrary",),
                                             vmem_limit_bytes=VMEM_LIMIT_BYTES),
        name=name,
    )(*[a for a, _ in streamed], *[a for a, _ in resident], *[w for w, _ in casts])
    return outs[0], outs[1:]


def kernel(x, p, ev_w_in, ev_pool_w, ev_pool_scale, ev_sgu_ln_g, ev_sgu_ln_b, ev_sgu_w, ev_sgu_b,
           ev_w_out, od_w_in, od_conv_w, od_w_out, ln1_g, ln1_b, ffn_w1, ffn_w2, ln2_g, ln2_b,
           ple_gate_w, ple_w):
    assert x.shape[1] % SEQ_TILE == 0 and SEQ_TILE % SUB_ROWS == 0 and SUB_ROWS % CHUNK == 0
    assert ln1_g.shape[0] == DEPTH and ev_w_in.shape[0] == 1 and od_w_in.shape[0] == 1
    bsz, _, d = x.shape
    p_spec = lambda i: pl.BlockSpec((None, bsz, SEQ_TILE, p.shape[-1]), lambda s: (i, 0, s, 0))
    channel_casts = lambda w_out, i: [(w_out, 0), (ffn_w1, i), (ffn_w2, i), (ple_gate_w, i),
                                      (ple_w, i)]
    channel_scratch = [pltpu.VMEM((LHS_SLOTS, SUB_ROWS, d), _BF16),
                       pltpu.VMEM((LHS_SLOTS, SUB_ROWS, ffn_w1.shape[-1]), _BF16),
                       pltpu.VMEM((LHS_SLOTS, SUB_ROWS, d), _BF16)]

    def channel(i, x, y, weights, casts):
        w_out, w1, w2, wg, wp = weights
        resident = [(w_out, None), (ln1_g, None), (ln1_b, None), (w1, None), (w2, None),
                    (ln2_g, None), (ln2_b, None), (wg, None), (wp, None)]
        return _call(f"channel_{i}", functools.partial(_channel_kernel, layer=i),
                     [(x, None), (y, None), (p, p_spec(i))], resident, x.dtype,
                     channel_scratch, casts)

    resident = [(ev_w_in, 0), (ev_pool_w, 0), (ev_pool_scale, None), (ev_sgu_ln_g, None),
                (ev_sgu_ln_b, None), (ev_sgu_w, 0), (ev_sgu_b, 0)]
    pool_buf = pltpu.VMEM((bsz, POOL_BASE + SEQ_TILE, POOL_WIDTH), _F32)
    scratch = [pltpu.VMEM(ev_w_in.shape[1:], _BF16),
               pltpu.VMEM((2, 2 * POOL_GROUP_DIM, 2 * POOL_GROUP_DIM), _BF16),
               pltpu.VMEM((SGU_HEADS, CHUNK, CHUNK), _BF16),
               pltpu.VMEM((CHUNK, SGU_WIDTH), _F32),
               pool_buf, pool_buf, pool_buf, pool_buf]
    y, weights = _call("even_mixer", _even_mixer_kernel, [(x, None)], resident, _BF16, scratch,
                       channel_casts(ev_w_out, 0))
    x, (w_in, *weights) = channel(0, x, y, weights, [(od_w_in, 0)] + channel_casts(od_w_out, 1))

    scratch = [pltpu.VMEM((bsz, CONV_HALO + SEQ_TILE, od_conv_w.shape[-1]), _F32)]
    y, _ = _call("odd_mixer", _odd_mixer_kernel, [(x, None)],
                 [(w_in, None), (od_conv_w.reshape(od_conv_w.shape[1:]), None)],
                 _BF16, scratch, [])
    x, _ = channel(1, x, y, weights, [])
    return x
```

```python
import functools

import jax
import jax.numpy as jnp
from jax import lax
from jax.experimental import pallas as pl
from jax.experimental.pallas import tpu as pltpu

POOL_WINDOWS = (2, 4, 8, 16)
POOL_GROUP_DIM = 128
POOL_WIDTH = POOL_GROUP_DIM * len(POOL_WINDOWS)
SGU_HEADS = 4
SGU_HEAD_DIM = 128
SGU_WIDTH = SGU_HEADS * SGU_HEAD_DIM
CHUNK = 128
CONV_WIDTH = 3
LN_EPS = 1e-5
DEPTH = 2
DEEPNORM_ALPHA = (2.0 * DEPTH) ** 0.25

SUBLANES = 8
POOL_HALO = max(POOL_WINDOWS)
POOL_PAD = SUBLANES
POOL_BASE = POOL_PAD + POOL_HALO
CONV_HALO = SUBLANES

SEQ_TILE = 512
SUB_ROWS = 256
LHS_SLOTS = 2
VMEM_LIMIT_BYTES = 56 * 1024 * 1024

_BF16 = jnp.bfloat16
_F32 = jnp.float32


def _dot(a, b):
    return jnp.dot(a, b, preferred_element_type=_F32)


def _layer_norm(x, g, b):
    mu = jnp.mean(x, axis=-1, keepdims=True)
    xc = x - mu
    var = jnp.mean(xc * xc, axis=-1, keepdims=True)
    return xc * lax.rsqrt(var + LN_EPS) * g + b


def _chains(x_ref):
    bsz, ts, _ = x_ref.shape
    return [(b, r) for b in range(bsz) for r in range(0, ts, SUB_ROWS)]


def _layer_row(ref, layer):
    return ref[layer:layer + 1, :]


def _even_mixer_kernel(x_ref, w_in_f32_ref, pool_w_ref, pool_scale_ref, sgu_g_ref, sgu_b_ref,
                       sgu_w_ref, sgu_pos_b_ref, y_ref, w_in_ref, w_tri_ref,
                       sgu_bias_ref, ext_ref, lvl2_ref, lvl4_ref, lvl8_ref):
    s = pl.program_id(0)
    bsz, ts, _ = x_ref.shape
    sub, g, d, base = SUB_ROWS, POOL_GROUP_DIM, SGU_HEAD_DIM, POOL_BASE
    chains = _chains(x_ref)

    @pl.when(s == 0)
    def _():
        for k in range(len(POOL_WINDOWS)):
            cols = slice(k * g, (k + 1) * g)
            folded = jnp.dot(w_in_f32_ref[:, cols], pool_w_ref[k], precision=lax.Precision.HIGHEST,
                             preferred_element_type=_F32)
            w_in_ref[:, cols] = (folded * pool_scale_ref[:, cols]).astype(_BF16)
        w_in_ref[:, POOL_WIDTH:] = w_in_f32_ref[:, POOL_WIDTH:].astype(_BF16)
        tri =(lax.broadcasted_iota(jnp.int32, (CHUNK, CHUNK), 0)
               >= lax.broadcasted_iota(jnp.int32, (CHUNK, CHUNK), 1))
        for h in range(SGU_HEADS):
            w_tri_ref[h] = jnp.where(tri, sgu_w_ref[h], 0.0).astype(_BF16)
            pos_b = jnp.broadcast_to(sgu_pos_b_ref[h:h + 1, :], (SGU_HEAD_DIM, CHUNK))
            sgu_bias_ref[:, h * d:(h + 1) * d] = pos_b.T
        for b in range(bsz):
            ext_ref[b, 0:base, :] = jnp.zeros((base, POOL_WIDTH), _F32)
            lvl2_ref[b, 0:POOL_PAD, :] = jnp.zeros((POOL_PAD, POOL_WIDTH), _F32)
            lvl4_ref[b, 0:POOL_PAD, :] = jnp.zeros((POOL_PAD, POOL_WIDTH), _F32)

    a, u, v, vn = {}, {}, {}, {}
    for ch in chains:
        b, r = ch
        xb = x_ref[b, r:r + sub, :].astype(_BF16)
        a[ch] = _dot(xb, w_in_ref[:, 0:POOL_WIDTH])
        ext_ref[b, base + r:base + r + sub, :] = a[ch]
        u[ch] = _dot(xb, w_in_ref[:, POOL_WIDTH:POOL_WIDTH + SGU_WIDTH])
        v[ch] = _dot(xb, w_in_ref[:, POOL_WIDTH + SGU_WIDTH:])

    for ch in chains:
        b, r = ch
        lo = POOL_PAD if r == 0 else base + r
        hi = base + r + sub
        own = slice(base + r, hi)
        lvl2_ref[b, lo:hi, :] = ext_ref[b, lo:hi, :] + ext_ref[b, lo - 1:hi - 1, :]
        lvl4_ref[b, lo:hi, g:] = lvl2_ref[b, lo:hi, g:] + lvl2_ref[b, lo - 2:hi - 2, g:]
        lvl8_ref[b, lo:hi, 2 * g:] = lvl4_ref[b, lo:hi, 2 * g:] + lvl4_ref[b, lo - 4:hi - 4, 2 * g:]
        sums = (
            lvl2_ref[b, own, 0:g],
            lvl4_ref[b, own, g:2 * g],
            lvl8_ref[b, own, 2 * g:3 * g],
            lvl8_ref[b, own, 3 * g:4 * g] + lvl8_ref[b, base + r - 8:hi - 8, 3 * g:4 * g],
        )
        t1 = lax.broadcasted_iota(jnp.int32, (sub, g), 0) + (s * ts + r + 1)
        for gi, w in enumerate(POOL_WINDOWS):
            cnt = jnp.minimum(t1, w).astype(_F32)
            pooled = sums[gi] / cnt - a[ch][:, gi * g:(gi + 1) * g]
            y_ref[b, r:r + sub, gi * g:(gi + 1) * g] = pooled.astype(_BF16)
        vn[ch] = _layer_norm(v[ch], sgu_g_ref[...], sgu_b_ref[...]).astype(_BF16)

    for b in range(bsz):
        ext_ref[b, POOL_PAD:base, :] = a[(b, ts - sub)][sub - POOL_HALO:sub, :]

    n_chunks = sub // CHUNK
    for ch in chains:
        b, r = ch
        for h in range(SGU_HEADS):
            rhs = jnp.concatenate(
                [vn[ch][c * CHUNK:(c + 1) * CHUNK, h * d:(h + 1) * d] for c in range(n_chunks)],
                axis=1)
            mixed = _dot(w_tri_ref[h], rhs)
            bias_h = sgu_bias_ref[:, h * d:(h + 1) * d]
            for c in range(n_chunks):
                m = mixed[:, c * d:(c + 1) * d] + bias_h
                yb = u[ch][c * CHUNK:(c + 1) * CHUNK, h * d:(h + 1) * d] * m
                y_ref[b, r + c * CHUNK:r + (c + 1) * CHUNK,
                      POOL_WIDTH + h * d:POOL_WIDTH + (h + 1) * d] = yb.astype(_BF16)


def _odd_mixer_kernel(x_ref, w_in_ref, conv_w_ref, y_ref, z_ref):
    s = pl.program_id(0)
    bsz, ts, _ = x_ref.shape
    sub = SUB_ROWS
    cd = conv_w_ref.shape[-1]
    chains = _chains(x_ref)

    @pl.when(s == 0)
    def _():
        for b in range(bsz):
            z_ref[b, 0:CONV_HALO, :] = jnp.zeros((CONV_HALO, cd), _F32)

    z, gate_b = {}, {}
    for ch in chains:
        b, r = ch
        xb = x_ref[b, r:r + sub, :].astype(_BF16)
        gate_c = _dot(xb, w_in_ref[:, cd:2 * cd])
        hid = _dot(xb, w_in_ref[:, 2 * cd:3 * cd])
        z[ch] = gate_c * hid
        z_ref[b, CONV_HALO + r:CONV_HALO + r + sub, :] = z[ch]
        gate_b[ch] = _dot(xb, w_in_ref[:, 0:cd])
    cw = conv_w_ref[...]
    for ch in chains:
        b, r = ch
        conv = cw[CONV_WIDTH - 1:CONV_WIDTH, :] * z[ch]
        for k in range(CONV_WIDTH - 1):
            off = CONV_HALO + r - (CONV_WIDTH - 1) + k
            conv = conv + cw[k:k + 1, :] * z_ref[b, off:off + sub, :]
        y_ref[b, r:r + sub, :] = (gate_b[ch] * conv).astype(_BF16)
    for b in range(bsz):
        z_ref[b, 0:CONV_HALO, :] = z[(b, ts - sub)][sub - CONV_HALO:sub, :]


def _channel_kernel(x_ref, y_ref, p_ref, w_out_ref, ln1_g_ref, ln1_b_ref, w1_ref, w2_ref,
                    ln2_g_ref, ln2_b_ref, wg_ref, wp_ref, o_ref, xb_ref, hid_ref, x2b_ref,
                    *, layer):
    sub = SUB_ROWS
    gain1, bias1 = _layer_row(ln1_g_ref, layer), _layer_row(ln1_b_ref, layer)
    gain2, bias2 = _layer_row(ln2_g_ref, layer), _layer_row(ln2_b_ref, layer)
    d_ff = w1_ref.shape[1]
    ff_chunk = w1_ref.shape[0]
    chains = _chains(x_ref)

    def out_proj(ch):
        b, r = ch
        return _dot(y_ref[b, r:r + sub, :], w_out_ref[...])

    def mlp(ch, slot, mix):
        b, r = ch
        x1 = _layer_norm(DEEPNORM_ALPHA * x_ref[b, r:r + sub, :] + mix, gain1, bias1)
        xb_ref[slot] = x1.astype(_BF16)
        for c in range(d_ff // ff_chunk):
            h = _dot(xb_ref[slot], w1_ref[:, c * ff_chunk:(c + 1) * ff_chunk])
            h = jnp.maximum(h, 0.0)
            hid_ref[slot, :, c * ff_chunk:(c + 1) * ff_chunk] = (h * h).astype(_BF16)
        acc = _dot(hid_ref[slot], w2_ref[...])
        return _layer_norm(DEEPNORM_ALPHA * x1 + acc, gain2, bias2)

    def embed_gate(ch, slot, x2):
        b, r = ch
        x2b_ref[slot] = x2.astype(_BF16)
        gate = jax.nn.sigmoid(_dot(x2b_ref[slot], wg_ref[...]))
        pp = _dot(p_ref[b, r:r + sub, :].astype(_BF16), wp_ref[...])
        o_ref[b, r:r + sub, :] = x2 + gate * pp

    mix = out_proj(chains[0])
    pending = None
    for i, ch in enumerate(chains):
        next_mix = out_proj(chains[i + 1]) if i + 1 < len(chains) else None
        x2 = mlp(ch, i % LHS_SLOTS, mix)
        if pending is not None:
            embed_gate(*pending)
        pending = (ch, i % LHS_SLOTS, x2)
        mix = next_mix
    embed_gate(*pending)


def _with_weight_casts(body, n_in, n_cast):
    def kern(*refs):
        ins, srcs = refs[:n_in], refs[n_in:n_in + n_cast]
        out = refs[n_in + n_cast]
        dsts = refs[n_in + n_cast + 1:n_in + 2 * n_cast + 1]
        scratch = refs[n_in + 2 * n_cast + 1:]
        body(*ins, out, *scratch)
        for src, dst in zip(srcs, dsts):
            dst[...] = src[...].astype(_BF16)
    return kern


def _resident(arr, index=None):
    if index is None:
        zeros = (0,) * arr.ndim
        return pl.BlockSpec(arr.shape, lambda s: zeros, pipeline_mode=pl.Buffered(1))
    zeros = (0,) * (arr.ndim - 1)
    return pl.BlockSpec((None,) + arr.shape[1:], lambda s: (index,) + zeros,
                        pipeline_mode=pl.Buffered(1))


def _call(name, body, streamed, resident, out_dtype, scratch, casts):
    bsz, seq, d = streamed[0][0].shape
    ts = SEQ_TILE
    steps = seq // ts
    act = pl.BlockSpec((bsz, ts, d), lambda s: (0, s, 0))
    cast_in, cast_out, cast_shapes = [], [], []
    for w, idx in casts:
        _, rows, cols = w.shape
        slab = max(rows // steps, 2 * SUBLANES)
        last = rows // slab - 1
        assert slab * (last + 1) == rows and slab % (2 * SUBLANES) == 0 and last < steps
        cast_in.append(pl.BlockSpec((None, slab, cols),
                                    lambda s, idx=idx, last=last: (idx, jnp.minimum(s, last), 0)))
        cast_out.append(pl.BlockSpec((slab, cols), lambda s, last=last: (jnp.minimum(s, last), 0)))
        cast_shapes.append(jax.ShapeDtypeStruct((rows, cols), _BF16))
    n_in = len(streamed) + len(resident)
    outs = pl.pallas_call(
        _with_weight_casts(body, n_in, len(casts)),
        grid=(steps,),
        in_specs=([act if spec is None else spec for _, spec in streamed]
                  + [_resident(a, idx) for a, idx in resident] + cast_in),
        out_specs=[act] + cast_out,
        out_shape=[jax.ShapeDtypeStruct((bsz, seq, d), out_dtype)] + cast_shapes,
        scratch_shapes=scratch,
        compiler_params=pltpu.CompilerParams(dimension_semantics=("arbitrary",),
                                             vmem_limit_bytes=VMEM_LIMIT_BYTES),
        name=name,
    )(*[a for a, _ in streamed], *[a for a, _ in resident], *[w for w, _ in casts])
    return outs[0], outs[1:]


def kernel(x, p, ev_w_in, ev_pool_w, ev_pool_scale, ev_sgu_ln_g, ev_sgu_ln_b, ev_sgu_w, ev_sgu_b,
           ev_w_out, od_w_in, od_conv_w, od_w_out, ln1_g, ln1_b, ffn_w1, ffn_w2, ln2_g, ln2_b,
           ple_gate_w, ple_w):
    assert x.shape[1] % SEQ_TILE == 0 and SEQ_TILE % SUB_ROWS == 0 and SUB_ROWS % CHUNK == 0
    assert ln1_g.shape[0] == DEPTH and ev_w_in.shape[0] == 1 and od_w_in.shape[0] == 1
    bsz, _, d = x.shape
    p_spec = lambda i: pl.BlockSpec((None, bsz, SEQ_TILE, p.shape[-1]), lambda s: (i, 0, s, 0))
    channel_casts = lambda w_out, i: [(w_out, 0), (ffn_w1, i), (ffn_w2, i), (ple_gate_w, i),
                                      (ple_w, i)]
    channel_scratch = [pltpu.VMEM((LHS_SLOTS, SUB_ROWS, d), _BF16),
                       pltpu.VMEM((LHS_SLOTS, SUB_ROWS, ffn_w1.shape[-1]), _BF16),
                       pltpu.VMEM((LHS_SLOTS, SUB_ROWS, d), _BF16)]

    def channel(i, x, y, weights, casts):
        w_out, w1, w2, wg, wp = weights
        resident = [(w_out, None), (ln1_g, None), (ln1_b, None), (w1, None), (w2, None),
                    (ln2_g, None), (ln2_b, None), (wg, None), (wp, None)]
        return _call(f"channel_{i}", functools.partial(_channel_kernel, layer=i),
                     [(x, None), (y, None), (p, p_spec(i))], resident, x.dtype,
                     channel_scratch, casts)

    resident = [(ev_w_in, 0), (ev_pool_w, 0), (ev_pool_scale, None), (ev_sgu_ln_g, None),
                (ev_sgu_ln_b, None), (ev_sgu_w, 0), (ev_sgu_b, 0)]
    pool_buf = pltpu.VMEM((bsz, POOL_BASE + SEQ_TILE, POOL_WIDTH), _F32)
    scratch = [pltpu.VMEM(ev_w_in.shape[1:], _BF16),
               pltpu.VMEM((SGU_HEADS, CHUNK, CHUNK), _BF16),
               pltpu.VMEM((CHUNK, SGU_WIDTH), _F32),
               pool_buf, pool_buf, pool_buf, pool_buf]
    y, weights = _call("even_mixer", _even_mixer_kernel, [(x, None)], resident, _BF16, scratch,
                       channel_casts(ev_w_out, 0))
    x, (w_in,) = channel(0, x, y, weights, [(od_w_in, 0)])

    scratch = [pltpu.VMEM((bsz, CONV_HALO + SEQ_TILE, od_conv_w.shape[-1]), _F32)]
    y, weights = _call("odd_mixer", _odd_mixer_kernel, [(x, None)],
                       [(w_in, None), (od_conv_w, 0)], _BF16, scratch,
                       channel_casts(od_w_out, 1))
    x, _ = channel(1, x, y, weights, [])
    return x
```

```python
import functools

import jax
import jax.numpy as jnp
from jax import lax
from jax.experimental import pallas as pl
from jax.experimental.pallas import tpu as pltpu

POOL_WINDOWS = (2, 4, 8, 16)
POOL_GROUP_DIM = 128
POOL_WIDTH = POOL_GROUP_DIM * len(POOL_WINDOWS)
SGU_HEADS = 4
SGU_HEAD_DIM = 128
SGU_WIDTH = SGU_HEADS * SGU_HEAD_DIM
CHUNK = 128
CONV_WIDTH = 3
LN_EPS = 1e-5
DEPTH = 2
DEEPNORM_ALPHA = (2.0 * DEPTH) ** 0.25

SUBLANES = 8
POOL_HALO = max(POOL_WINDOWS)
POOL_PAD = SUBLANES
POOL_BASE = POOL_PAD + POOL_HALO
CONV_HALO = SUBLANES

SEQ_TILE = 512
SUB_ROWS = 256
LHS_SLOTS = 2
VMEM_LIMIT_BYTES = 56 * 1024 * 1024
CHANNEL_VMEM_LIMIT_BYTES = 60 * 1024 * 1024
STREAM_BUFFERS = 3

_BF16 = jnp.bfloat16
_F32 = jnp.float32


def _dot(a, b):
    return jnp.dot(a, b, preferred_element_type=_F32)


def _layer_norm(x, g, b):
    mu = jnp.mean(x, axis=-1, keepdims=True)
    xc = x - mu
    var = jnp.mean(xc * xc, axis=-1, keepdims=True)
    return xc * lax.rsqrt(var + LN_EPS) * g + b


def _chains(x_ref):
    bsz, ts, _ = x_ref.shape
    return [(b, r) for b in range(bsz) for r in range(0, ts, SUB_ROWS)]


def _layer_row(ref, layer):
    return ref[layer:layer + 1, :]


def _even_mixer_kernel(x_ref, w_in_f32_ref, pool_w_ref, pool_scale_ref, sgu_g_ref, sgu_b_ref,
                       sgu_w_ref, sgu_pos_b_ref, y_ref, w_in_ref, pool_bd_ref, w_tri_ref,
                       sgu_bias_ref, ext_ref, lvl2_ref, lvl4_ref, lvl8_ref):
    s = pl.program_id(0)
    bsz, ts, _ = x_ref.shape
    sub, g, d, base = SUB_ROWS, POOL_GROUP_DIM, SGU_HEAD_DIM, POOL_BASE
    chains = _chains(x_ref)

    @pl.when(s == 0)
    def _():
        w_in_ref[...] = w_in_f32_ref[...].astype(_BF16)
        pool_bd_ref[...] = jnp.zeros(pool_bd_ref.shape, _BF16)
        for k in range(len(POOL_WINDOWS)):
            off = (k % 2) * g
            pool_bd_ref[k // 2, off:off + g, off:off + g] = pool_w_ref[k].astype(_BF16)
        tri = (lax.broadcasted_iota(jnp.int32, (CHUNK, CHUNK), 0)
               >= lax.broadcasted_iota(jnp.int32, (CHUNK, CHUNK), 1))
        for h in range(SGU_HEADS):
            w_tri_ref[h] = jnp.where(tri, sgu_w_ref[h], 0.0).astype(_BF16)
            pos_b = jnp.broadcast_to(sgu_pos_b_ref[h:h + 1, :], (SGU_HEAD_DIM, CHUNK))
            sgu_bias_ref[:, h * d:(h + 1) * d] = pos_b.T
        for b in range(bsz):
            ext_ref[b, 0:base, :] = jnp.zeros((base, POOL_WIDTH), _F32)
            lvl2_ref[b, 0:POOL_PAD, :] = jnp.zeros((POOL_PAD, POOL_WIDTH), _F32)
            lvl4_ref[b, 0:POOL_PAD, :] = jnp.zeros((POOL_PAD, POOL_WIDTH), _F32)

    a, u, v, pooled, vn = {}, {}, {}, {}, {}
    for ch in chains:
        b, r = ch
        xb = x_ref[b, r:r + sub, :].astype(_BF16)
        a[ch] = _dot(xb, w_in_ref[:, 0:POOL_WIDTH])
        ext_ref[b, base + r:base + r + sub, :] = a[ch]
        u[ch] = _dot(xb, w_in_ref[:, POOL_WIDTH:POOL_WIDTH + SGU_WIDTH])
        v[ch] = _dot(xb, w_in_ref[:, POOL_WIDTH + SGU_WIDTH:])

    for ch in chains:
        b, r = ch
        lo = POOL_PAD if r == 0 else base + r
        hi = base + r + sub
        own = slice(base + r, hi)
        lvl2_ref[b, lo:hi, :] = ext_ref[b, lo:hi, :] + ext_ref[b, lo - 1:hi - 1, :]
        lvl4_ref[b, lo:hi, g:] = lvl2_ref[b, lo:hi, g:] + lvl2_ref[b, lo - 2:hi - 2, g:]
        lvl8_ref[b, lo:hi, 2 * g:] = lvl4_ref[b, lo:hi, 2 * g:] + lvl4_ref[b, lo - 4:hi - 4, 2 * g:]
        sums = (
            lvl2_ref[b, own, 0:g],
            lvl4_ref[b, own, g:2 * g],
            lvl8_ref[b, own, 2 * g:3 * g],
            lvl8_ref[b, own, 3 * g:4 * g] + lvl8_ref[b, base + r - 8:hi - 8, 3 * g:4 * g],
        )
        t1 = lax.broadcasted_iota(jnp.int32, (sub, g), 0) + (s * ts + r + 1)
        parts = []
        for gi, w in enumerate(POOL_WINDOWS):
            cnt = jnp.minimum(t1, w).astype(_F32)
            parts.append((sums[gi] / cnt - a[ch][:, gi * g:(gi + 1) * g]).astype(_BF16))
        pooled[ch] = parts
        vn[ch] = _layer_norm(v[ch], sgu_g_ref[...], sgu_b_ref[...]).astype(_BF16)

    for b in range(bsz):
        ext_ref[b, POOL_PAD:base, :] = a[(b, ts - sub)][sub - POOL_HALO:sub, :]

    scale = pool_scale_ref[...]
    n_chunks = sub // CHUNK
    for ch in chains:
        b, r = ch
        for pair in range(2):
            pb = jnp.concatenate(pooled[ch][2 * pair:2 * pair + 2], axis=1)
            ya = _dot(pb, pool_bd_ref[pair]) * scale[:, 2 * g * pair:2 * g * (pair + 1)]
            y_ref[b, r:r + sub, 2 * g * pair:2 * g * (pair + 1)] = ya.astype(_BF16)
        for h in range(SGU_HEADS):
            rhs = jnp.concatenate(
                [vn[ch][c * CHUNK:(c + 1) * CHUNK, h * d:(h + 1) * d] for c in range(n_chunks)],
                axis=1)
            mixed = _dot(w_tri_ref[h], rhs)
            bias_h = sgu_bias_ref[:, h * d:(h + 1) * d]
            for c in range(n_chunks):
                m = mixed[:, c * d:(c + 1) * d] + bias_h
                yb = u[ch][c * CHUNK:(c + 1) * CHUNK, h * d:(h + 1) * d] * m
                y_ref[b, r + c * CHUNK:r + (c + 1) * CHUNK,
                      POOL_WIDTH + h * d:POOL_WIDTH + (h + 1) * d] = yb.astype(_BF16)


def _odd_mixer_kernel(x_ref, w_in_ref, conv_w_ref, y_ref, z_ref):
    s = pl.program_id(0)
    bsz, ts, _ = x_ref.shape
    sub = SUB_ROWS
    cd = conv_w_ref.shape[-1]
    chains = _chains(x_ref)

    @pl.when(s == 0)
    def _():
        for b in range(bsz):
            z_ref[b, 0:CONV_HALO, :] = jnp.zeros((CONV_HALO, cd), _F32)

    z, gate_b = {}, {}
    for ch in chains:
        b, r = ch
        xb = x_ref[b, r:r + sub, :].astype(_BF16)
        gate_c = _dot(xb, w_in_ref[:, cd:2 * cd])
        hid = _dot(xb, w_in_ref[:, 2 * cd:3 * cd])
        z[ch] = gate_c * hid
        z_ref[b, CONV_HALO + r:CONV_HALO + r + sub, :] = z[ch]
        gate_b[ch] = _dot(xb, w_in_ref[:, 0:cd])
    cw = conv_w_ref[...]
    for ch in chains:
        b, r = ch
        conv = cw[CONV_WIDTH - 1:CONV_WIDTH, :] * z[ch]
        for k in range(CONV_WIDTH - 1):
            off = CONV_HALO + r - (CONV_WIDTH - 1) + k
            conv = conv + cw[k:k + 1, :] * z_ref[b, off:off + sub, :]
        y_ref[b, r:r + sub, :] = (gate_b[ch] * conv).astype(_BF16)
    for b in range(bsz):
        z_ref[b, 0:CONV_HALO, :] = z[(b, ts - sub)][sub - CONV_HALO:sub, :]


def _channel_kernel(x_ref, y_ref, p_ref, w_out_ref, ln1_g_ref, ln1_b_ref, w1_ref, w2_ref,
                    ln2_g_ref, ln2_b_ref, wg_ref, wp_ref, o_ref, xb_ref, hid_ref, x2b_ref,
                    *, layer):
    sub = SUB_ROWS
    gain1, bias1 = _layer_row(ln1_g_ref, layer), _layer_row(ln1_b_ref, layer)
    gain2, bias2 = _layer_row(ln2_g_ref, layer), _layer_row(ln2_b_ref, layer)
    d_ff = w1_ref.shape[1]
    ff_chunk = w1_ref.shape[0]
    chains = _chains(x_ref)

    def out_proj(ch):
        b, r = ch
        return _dot(y_ref[b, r:r + sub, :], w_out_ref[...])

    def mlp(ch, slot, mix):
        b, r = ch
        x1 = _layer_norm(DEEPNORM_ALPHA * x_ref[b, r:r + sub, :] + mix, gain1, bias1)
        xb_ref[slot] = x1.astype(_BF16)
        for c in range(d_ff // ff_chunk):
            h = _dot(xb_ref[slot], w1_ref[:, c * ff_chunk:(c + 1) * ff_chunk])
            h = jnp.maximum(h, 0.0)
            hid_ref[slot, :, c * ff_chunk:(c + 1) * ff_chunk] = (h * h).astype(_BF16)
        acc = _dot(hid_ref[slot], w2_ref[...])
        return _layer_norm(DEEPNORM_ALPHA * x1 + acc, gain2, bias2)

    def embed_gate(ch, slot, x2):
        b, r = ch
        x2b_ref[slot] = x2.astype(_BF16)
        gate = jax.nn.sigmoid(_dot(x2b_ref[slot], wg_ref[...]))
        pp = _dot(p_ref[b, r:r + sub, :].astype(_BF16), wp_ref[...])
        o_ref[b, r:r + sub, :] = x2 + gate * pp

    mix = out_proj(chains[0])
    pending = None
    for i, ch in enumerate(chains):
        next_mix = out_proj(chains[i + 1]) if i + 1 < len(chains) else None
        x2 = mlp(ch, i % LHS_SLOTS, mix)
        if pending is not None:
            embed_gate(*pending)
        pending = (ch, i % LHS_SLOTS, x2)
        mix = next_mix
    embed_gate(*pending)


def _with_weight_casts(body, n_in, n_cast):
    def kern(*refs):
        ins, srcs = refs[:n_in], refs[n_in:n_in + n_cast]
        out = refs[n_in + n_cast]
        dsts = refs[n_in + n_cast + 1:n_in + 2 * n_cast + 1]
        scratch = refs[n_in + 2 * n_cast + 1:]
        body(*ins, out, *scratch)
        for src, dst in zip(srcs, dsts):
            dst[...] = src[...].astype(_BF16)
    return kern


def _resident(arr, index=None):
    if index is None:
        zeros = (0,) * arr.ndim
        return pl.BlockSpec(arr.shape, lambda s: zeros, pipeline_mode=pl.Buffered(1))
    zeros = (0,) * (arr.ndim - 1)
    return pl.BlockSpec((None,) + arr.shape[1:], lambda s: (index,) + zeros,
                        pipeline_mode=pl.Buffered(1))


def _call(name, body, streamed, resident, out_dtype, scratch, casts):
    bsz, seq, d = streamed[0][0].shape
    ts = SEQ_TILE
    steps = seq // ts
    act = pl.BlockSpec((bsz, ts, d), lambda s: (0, s, 0))
    cast_in, cast_out, cast_shapes = [], [], []
    for w, idx in casts:
        _, rows, cols = w.shape
        slab = max(rows // steps, 2 * SUBLANES)
        last = rows // slab - 1
        assert slab * (last + 1) == rows and slab % (2 * SUBLANES) == 0 and last < steps
        cast_in.append(pl.BlockSpec((None, slab, cols),
                                    lambda s, idx=idx, last=last: (idx, jnp.minimum(s, last), 0)))
        cast_out.append(pl.BlockSpec((slab, cols), lambda s, last=last: (jnp.minimum(s, last), 0)))
        cast_shapes.append(jax.ShapeDtypeStruct((rows, cols), _BF16))
    n_in = len(streamed) + len(resident)
    outs = pl.pallas_call(
        _with_weight_casts(body, n_in, len(casts)),
        grid=(steps,),
        in_specs=([act if spec is None else spec for _, spec in streamed]
                  + [_resident(a, idx) for a, idx in resident] + cast_in),
        out_specs=[act] + cast_out,
        out_shape=[jax.ShapeDtypeStruct((bsz, seq, d), out_dtype)] + cast_shapes,
        scratch_shapes=scratch,
        compiler_params=pltpu.CompilerParams(dimension_semantics=("arbitrary",),
                                             vmem_limit_bytes=VMEM_LIMIT_BYTES),
        name=name,
    )(*[a for a, _ in streamed], *[a for a, _ in resident], *[w for w, _ in casts])
    return outs[0], outs[1:]


def kernel(x, p, ev_w_in, ev_pool_w, ev_pool_scale, ev_sgu_ln_g, ev_sgu_ln_b, ev_sgu_w, ev_sgu_b,
           ev_w_out, od_w_in, od_conv_w, od_w_out, ln1_g, ln1_b, ffn_w1, ffn_w2, ln2_g, ln2_b,
           ple_gate_w, ple_w):
    assert x.shape[1] % SEQ_TILE == 0 and SEQ_TILE % SUB_ROWS == 0 and SUB_ROWS % CHUNK == 0
    assert ln1_g.shape[0] == DEPTH and ev_w_in.shape[0] == 1 and od_w_in.shape[0] == 1
    bsz, _, d = x.shape
    p_spec = lambda i: pl.BlockSpec((None, bsz, SEQ_TILE, p.shape[-1]), lambda s: (i, 0, s, 0))
    channel_casts = lambda w_out, i: [(w_out, 0), (ffn_w1, i), (ffn_w2, i), (ple_gate_w, i),
                                      (ple_w, i)]
    channel_scratch = [pltpu.VMEM((LHS_SLOTS, SUB_ROWS, d), _BF16),
                       pltpu.VMEM((LHS_SLOTS, SUB_ROWS, ffn_w1.shape[-1]), _BF16),
                       pltpu.VMEM((LHS_SLOTS, SUB_ROWS, d), _BF16)]

    def channel(i, x, y, weights, casts):
        w_out, w1, w2, wg, wp = weights
        resident = (w_out, ln1_g, ln1_b, w1, w2, ln2_g, ln2_b, wg, wp)
        seq = x.shape[1]
        steps = seq // SEQ_TILE
        deep = pl.Buffered(STREAM_BUFFERS)
        act_in = pl.BlockSpec((bsz, SEQ_TILE, d), lambda s: (0, s, 0), pipeline_mode=deep)
        p_in = pl.BlockSpec((bsz, SEQ_TILE, p.shape[-1]), lambda s: (0, s, 0), pipeline_mode=deep)
        act_out = pl.BlockSpec((bsz, SEQ_TILE, d), lambda s: (0, s, 0))
        slabs = [w.shape[1] // steps for w, _ in casts]
        cast_in = [pl.BlockSpec((slab, w.shape[2]), lambda s: (s, 0))
                   for slab, (w, _) in zip(slabs, casts)]
        n_res, n_cast = len(resident), len(casts)

        def outer(*refs):
            x_hbm, y_hbm, p_hbm = refs[:3]
            res = refs[3:3 + n_res]
            srcs = refs[3 + n_res:3 + n_res + n_cast]
            o_hbm = refs[3 + n_res + n_cast]
            dsts = refs[4 + n_res + n_cast:4 + n_res + 2 * n_cast]
            scratch = refs[4 + n_res + 2 * n_cast:]

            def step(*tiles):
                x_ref, y_ref, p_ref = tiles[:3]
                src_tiles = tiles[3:3 + n_cast]
                o_ref = tiles[3 + n_cast]
                dst_tiles = tiles[4 + n_cast:]
                _channel_kernel(x_ref, y_ref, p_ref, *res, o_ref, *scratch, layer=i)
                for src, dst in zip(src_tiles, dst_tiles):
                    dst[...] = src[...].astype(_BF16)

            pltpu.emit_pipeline(step, grid=(steps,),
                                in_specs=[act_in, act_in, p_in] + cast_in,
                                out_specs=[act_out] + cast_in)(
                x_hbm, y_hbm, p_hbm.at[i], *[src.at[idx] for src, (_, idx) in zip(srcs, casts)],
                o_hbm, *dsts)

        hbm = pl.BlockSpec(memory_space=pl.ANY)
        vmem = pl.BlockSpec(memory_space=pltpu.VMEM)
        outs = pl.pallas_call(
            outer,
            in_specs=[hbm, hbm, hbm] + [vmem] * n_res + [hbm] * n_cast,
            out_specs=[hbm] * (1 + n_cast),
            out_shape=[jax.ShapeDtypeStruct(x.shape, x.dtype)]
            + [jax.ShapeDtypeStruct(w.shape[1:], _BF16) for w, _ in casts],
            scratch_shapes=channel_scratch,
            compiler_params=pltpu.CompilerParams(vmem_limit_bytes=CHANNEL_VMEM_LIMIT_BYTES),
            name=f"channel_{i}",
        )(x, y, p, *resident, *[w for w, _ in casts])
        return outs[0], outs[1:]

    resident = [(ev_w_in, 0), (ev_pool_w, 0), (ev_pool_scale, None), (ev_sgu_ln_g, None),
                (ev_sgu_ln_b, None), (ev_sgu_w, 0), (ev_sgu_b, 0)]
    pool_buf = pltpu.VMEM((bsz, POOL_BASE + SEQ_TILE, POOL_WIDTH), _F32)
    scratch = [pltpu.VMEM(ev_w_in.shape[1:], _BF16),
               pltpu.VMEM((2, 2 * POOL_GROUP_DIM, 2 * POOL_GROUP_DIM), _BF16),
               pltpu.VMEM((SGU_HEADS, CHUNK, CHUNK), _BF16),
               pltpu.VMEM((CHUNK, SGU_WIDTH), _F32),
               pool_buf, pool_buf, pool_buf, pool_buf]
    y, weights = _call("even_mixer", _even_mixer_kernel, [(x, None)], resident, _BF16, scratch,
                       channel_casts(ev_w_out, 0))
    x, (w_in,) = channel(0, x, y, weights, [(od_w_in, 0)])

    scratch = [pltpu.VMEM((bsz, CONV_HALO + SEQ_TILE, od_conv_w.shape[-1]), _F32)]
    y, weights = _call("odd_mixer", _odd_mixer_kernel, [(x, None)],
                       [(w_in, None), (od_conv_w, 0)], _BF16, scratch,
                       channel_casts(od_w_out, 1))
    x, _ = channel(1, x, y, weights, [])
    return x
```

```python
import jax
import jax.numpy as jnp
from jax import lax
from jax.experimental import pallas as pl
from jax.experimental.pallas import tpu as pltpu

POOL_WINDOWS = (2, 4, 8, 16)
POOL_GROUP_DIM = 128
POOL_WIDTH = POOL_GROUP_DIM * len(POOL_WINDOWS)
SGU_HEADS = 4
SGU_HEAD_DIM = 128
SGU_WIDTH = SGU_HEADS * SGU_HEAD_DIM
CHUNK = 128
CONV_WIDTH = 3
LN_EPS = 1e-5
DEPTH = 2
DEEPNORM_ALPHA = (2.0 * DEPTH) ** 0.25

SUBLANES = 8
POOL_HALO = max(POOL_WINDOWS)
POOL_PAD = SUBLANES
POOL_BASE = POOL_PAD + POOL_HALO
CONV_HALO = SUBLANES

SEQ_TILE = 512
SUB_ROWS = 256
LHS_SLOTS = 2
VMEM_LIMIT_BYTES = 60 * 1024 * 1024

_BF16 = jnp.bfloat16
_F32 = jnp.float32


def _dot(a, b):
    return jnp.dot(a, b, preferred_element_type=_F32)


def _layer_norm(x, g, b):
    mu = jnp.mean(x, axis=-1, keepdims=True)
    xc = x - mu
    var = jnp.mean(xc * xc, axis=-1, keepdims=True)
    return xc * lax.rsqrt(var + LN_EPS) * g + b


def _chains(x_ref):
    bsz, ts, _ = x_ref.shape
    return [(b, r) for b in range(bsz) for r in range(0, ts, SUB_ROWS)]


def _layer_row(ref, layer):
    return ref[layer:layer + 1, :]


def _even_mixer_prep(w_in_f32_ref, pool_w_ref, sgu_w_ref, sgu_pos_b_ref,
                     w_in_ref, pool_bd_ref, w_tri_ref, sgu_bias_ref, ext_ref, lvl2_ref, lvl4_ref):
    g, d, base = POOL_GROUP_DIM, SGU_HEAD_DIM, POOL_BASE
    w_in_ref[...] = w_in_f32_ref[...].astype(_BF16)
    pool_bd_ref[...] = jnp.zeros(pool_bd_ref.shape, _BF16)
    for k in range(len(POOL_WINDOWS)):
        off = (k % 2) * g
        pool_bd_ref[k // 2, off:off + g, off:off + g] = pool_w_ref[k].astype(_BF16)
    tri = (lax.broadcasted_iota(jnp.int32, (CHUNK, CHUNK), 0)
           >= lax.broadcasted_iota(jnp.int32, (CHUNK, CHUNK), 1))
    for h in range(SGU_HEADS):
        w_tri_ref[h] = jnp.where(tri, sgu_w_ref[h], 0.0).astype(_BF16)
        pos_b = jnp.broadcast_to(sgu_pos_b_ref[h:h + 1, :], (SGU_HEAD_DIM, CHUNK))
        sgu_bias_ref[:, h * d:(h + 1) * d] = pos_b.T
    for b in range(ext_ref.shape[0]):
        ext_ref[b, 0:base, :] = jnp.zeros((base, POOL_WIDTH), _F32)
        lvl2_ref[b, 0:POOL_PAD, :] = jnp.zeros((POOL_PAD, POOL_WIDTH), _F32)
        lvl4_ref[b, 0:POOL_PAD, :] = jnp.zeros((POOL_PAD, POOL_WIDTH), _F32)


def _even_mixer_kernel(s, x_ref, pool_scale_ref, sgu_g_ref, sgu_b_ref, y_ref,
                       w_in_ref, pool_bd_ref, w_tri_ref, sgu_bias_ref,
                       ext_ref, lvl2_ref, lvl4_ref, lvl8_ref):
    bsz, ts, _ = x_ref.shape
    sub, g, d, base = SUB_ROWS, POOL_GROUP_DIM, SGU_HEAD_DIM, POOL_BASE
    chains = _chains(x_ref)

    a, u, v, pooled, vn = {}, {}, {}, {}, {}
    for ch in chains:
        b, r = ch
        xb = x_ref[b, r:r + sub, :].astype(_BF16)
        a[ch] = _dot(xb, w_in_ref[:, 0:POOL_WIDTH])
        ext_ref[b, base + r:base + r + sub, :] = a[ch]
        u[ch] = _dot(xb, w_in_ref[:, POOL_WIDTH:POOL_WIDTH + SGU_WIDTH])
        v[ch] = _dot(xb, w_in_ref[:, POOL_WIDTH + SGU_WIDTH:])

    for ch in chains:
        b, r = ch
        lo = POOL_PAD if r == 0 else base + r
        hi = base + r + sub
        own = slice(base + r, hi)
        lvl2_ref[b, lo:hi, :] = ext_ref[b, lo:hi, :] + ext_ref[b, lo - 1:hi - 1, :]
        lvl4_ref[b, lo:hi, g:] = lvl2_ref[b, lo:hi, g:] + lvl2_ref[b, lo - 2:hi - 2, g:]
        lvl8_ref[b, lo:hi, 2 * g:] = lvl4_ref[b, lo:hi, 2 * g:] + lvl4_ref[b, lo - 4:hi - 4, 2 * g:]
        sums = (
            lvl2_ref[b, own, 0:g],
            lvl4_ref[b, own, g:2 * g],
            lvl8_ref[b, own, 2 * g:3 * g],
            lvl8_ref[b, own, 3 * g:4 * g] + lvl8_ref[b, base + r - 8:hi - 8, 3 * g:4 * g],
        )
        t1 = lax.broadcasted_iota(jnp.int32, (sub, g), 0) + (s * ts + r + 1)
        parts = []
        for gi, w in enumerate(POOL_WINDOWS):
            cnt = jnp.minimum(t1, w).astype(_F32)
            parts.append((sums[gi] / cnt - a[ch][:, gi * g:(gi + 1) * g]).astype(_BF16))
        pooled[ch] = parts
        vn[ch] = _layer_norm(v[ch], sgu_g_ref[...], sgu_b_ref[...]).astype(_BF16)

    for b in range(bsz):
        ext_ref[b, POOL_PAD:base, :] = a[(b, ts - sub)][sub - POOL_HALO:sub, :]

    scale = pool_scale_ref[...]
    n_chunks = sub // CHUNK
    for ch in chains:
        b, r = ch
        for pair in range(2):
            pb = jnp.concatenate(pooled[ch][2 * pair:2 * pair + 2], axis=1)
            ya = _dot(pb, pool_bd_ref[pair]) * scale[:, 2 * g * pair:2 * g * (pair + 1)]
            y_ref[b, r:r + sub, 2 * g * pair:2 * g * (pair + 1)] = ya.astype(_BF16)
        for h in range(SGU_HEADS):
            rhs = jnp.concatenate(
                [vn[ch][c * CHUNK:(c + 1) * CHUNK, h * d:(h + 1) * d] for c in range(n_chunks)],
                axis=1)
            mixed = _dot(w_tri_ref[h], rhs)
            bias_h = sgu_bias_ref[:, h * d:(h + 1) * d]
            for c in range(n_chunks):
                m = mixed[:, c * d:(c + 1) * d] + bias_h
                yb = u[ch][c * CHUNK:(c + 1) * CHUNK, h * d:(h + 1) * d] * m
                y_ref[b, r + c * CHUNK:r + (c + 1) * CHUNK,
                      POOL_WIDTH + h * d:POOL_WIDTH + (h + 1) * d] = yb.astype(_BF16)


def _odd_mixer_kernel(s, x_ref, w_in_ref, conv_w_ref, y_ref, z_ref):
    bsz, ts, _ = x_ref.shape
    sub = SUB_ROWS
    cd = conv_w_ref.shape[-1]
    chains = _chains(x_ref)

    @pl.when(s == 0)
    def _():
        for b in range(bsz):
            z_ref[b, 0:CONV_HALO, :] = jnp.zeros((CONV_HALO, cd), _F32)

    z, gate_b = {}, {}
    for ch in chains:
        b, r = ch
        xb = x_ref[b, r:r + sub, :].astype(_BF16)
        gate_c = _dot(xb, w_in_ref[:, cd:2 * cd])
        hid = _dot(xb, w_in_ref[:, 2 * cd:3 * cd])
        z[ch] = gate_c * hid
        z_ref[b, CONV_HALO + r:CONV_HALO + r + sub, :] = z[ch]
        gate_b[ch] = _dot(xb, w_in_ref[:, 0:cd])
    cw = conv_w_ref[...]
    for ch in chains:
        b, r = ch
        conv = cw[CONV_WIDTH - 1:CONV_WIDTH, :] * z[ch]
        for k in range(CONV_WIDTH - 1):
            off = CONV_HALO + r - (CONV_WIDTH - 1) + k
            conv = conv + cw[k:k + 1, :] * z_ref[b, off:off + sub, :]
        y_ref[b, r:r + sub, :] = (gate_b[ch] * conv).astype(_BF16)
    for b in range(bsz):
        z_ref[b, 0:CONV_HALO, :] = z[(b, ts - sub)][sub - CONV_HALO:sub, :]


def _channel_kernel(x_ref, y_ref, p_ref, w_out_ref, ln1_g_ref, ln1_b_ref, w1_ref, w2_ref,
                    ln2_g_ref, ln2_b_ref, wg_ref, wp_ref, o_ref, xb_ref, hid_ref, x2b_ref,
                    *, layer):
    sub = SUB_ROWS
    gain1, bias1 = _layer_row(ln1_g_ref, layer), _layer_row(ln1_b_ref, layer)
    gain2, bias2 = _layer_row(ln2_g_ref, layer), _layer_row(ln2_b_ref, layer)
    d_ff = w1_ref.shape[1]
    ff_chunk = w1_ref.shape[0]
    chains = _chains(x_ref)

    def out_proj(ch):
        b, r = ch
        return _dot(y_ref[b, r:r + sub, :], w_out_ref[...])

    def mlp(ch, slot, mix):
        b, r = ch
        x1 = _layer_norm(DEEPNORM_ALPHA * x_ref[b, r:r + sub, :] + mix, gain1, bias1)
        xb_ref[slot] = x1.astype(_BF16)
        for c in range(d_ff // ff_chunk):
            h = _dot(xb_ref[slot], w1_ref[:, c * ff_chunk:(c + 1) * ff_chunk])
            h = jnp.maximum(h, 0.0)
            hid_ref[slot, :, c * ff_chunk:(c + 1) * ff_chunk] = (h * h).astype(_BF16)
        acc = _dot(hid_ref[slot], w2_ref[...])
        return _layer_norm(DEEPNORM_ALPHA * x1 + acc, gain2, bias2)

    def embed_gate(ch, slot, x2):
        b, r = ch
        x2b_ref[slot] = x2.astype(_BF16)
        gate = jax.nn.sigmoid(_dot(x2b_ref[slot], wg_ref[...]))
        pp = _dot(p_ref[b, r:r + sub, :].astype(_BF16), wp_ref[...])
        o_ref[b, r:r + sub, :] = x2 + gate * pp

    mix = out_proj(chains[0])
    pending = None
    for i, ch in enumerate(chains):
        next_mix = out_proj(chains[i + 1]) if i + 1 < len(chains) else None
        x2 = mlp(ch, i % LHS_SLOTS, mix)
        if pending is not None:
            embed_gate(*pending)
        pending = (ch, i % LHS_SLOTS, x2)
        mix = next_mix
    embed_gate(*pending)


def _layer(layer, x, p, ln, mixer_w_hbm, mixer_small, mixer_scratch, mixer_prep, mixer_step,
           channel_w, handoff_w):
    bsz, seq, d = x.shape
    steps = seq // SEQ_TILE
    act = pl.BlockSpec((bsz, SEQ_TILE, d), lambda s: (0, s, 0))
    p_in = pl.BlockSpec((bsz, SEQ_TILE, p.shape[-1]), lambda s: (0, s, 0))
    casts = list(channel_w) + list(handoff_w)
    slabs = [w.shape[1] // steps for w, _ in casts]
    assert all(slab * steps == w.shape[1] and slab % (2 * SUBLANES) == 0
               for slab, (w, _) in zip(slabs, casts))
    cast_specs = [pl.BlockSpec((slab, w.shape[2]), lambda s: (s, 0))
                  for slab, (w, _) in zip(slabs, casts)]
    n_small, n_cw, n_hw = len(mixer_small), len(channel_w), len(handoff_w)
    d_ff = channel_w[1][0].shape[-1]
    channel_scratch = [pltpu.VMEM((LHS_SLOTS, SUB_ROWS, d), _BF16),
                       pltpu.VMEM((LHS_SLOTS, SUB_ROWS, d_ff), _BF16),
                       pltpu.VMEM((LHS_SLOTS, SUB_ROWS, d), _BF16)]

    def body(x_hbm, p_hbm, w_hbm, *refs):
        small = refs[:n_small]
        ln_refs = refs[n_small:n_small + 4]
        srcs = refs[n_small + 4:n_small + 4 + n_cw + n_hw]
        k = n_small + 4 + n_cw + n_hw
        o_hbm, y_hbm = refs[k:k + 2]
        handoff = refs[k + 2:k + 2 + n_hw]
        w_refs = refs[k + 2 + n_hw:]

        def mixer_phase(tile_ref, *scratch):
            mixer_prep(w_hbm, small, scratch)
            tile_ref[0] = 0

            def step(x_ref, *tiles):
                src_tiles = tiles[:n_cw + n_hw]
                y_ref = tiles[n_cw + n_hw]
                out_tiles = tiles[n_cw + n_hw + 1:]
                s = tile_ref[0]
                mixer_step(s, x_ref, small, y_ref, scratch)
                for src, dst, slab in zip(src_tiles[:n_cw], w_refs, slabs):
                    row = pl.multiple_of(s * slab, slab)
                    dst[pl.ds(row, slab), :] = src[...].astype(_BF16)
                for src, dst in zip(src_tiles[n_cw:], out_tiles):
                    dst[...] = src[...].astype(_BF16)
                tile_ref[0] = s + 1

            pltpu.emit_pipeline(step, grid=(steps,), in_specs=[act] + cast_specs,
                                out_specs=[act] + cast_specs[n_cw:])(
                x_hbm, *[src.at[idx] for src, (_, idx) in zip(srcs, casts)], y_hbm, *handoff)

        pl.run_scoped(mixer_phase, pltpu.SMEM((1,), jnp.int32), *mixer_scratch)

        def channel_phase(xb_ref, hid_ref, x2b_ref):
            w_out_ref, w1_ref, w2_ref, wg_ref, wp_ref = w_refs
            g1_ref, b1_ref, g2_ref, b2_ref = ln_refs

            def step(x_ref, y_ref, p_ref, o_ref):
                _channel_kernel(x_ref, y_ref, p_ref, w_out_ref, g1_ref, b1_ref, w1_ref, w2_ref,
                                g2_ref, b2_ref, wg_ref, wp_ref, o_ref, xb_ref, hid_ref, x2b_ref,
                                layer=layer)

            pltpu.emit_pipeline(step, grid=(steps,), in_specs=[act, act, p_in], out_specs=[act])(
                x_hbm, y_hbm, p_hbm.at[layer], o_hbm)

        pl.run_scoped(channel_phase, *channel_scratch)

    hbm = pl.BlockSpec(memory_space=pl.ANY)
    vmem = pl.BlockSpec(memory_space=pltpu.VMEM)
    outs = pl.pallas_call(
        body,
        in_specs=[hbm, hbm, hbm] + [vmem] * (n_small + 4) + [hbm] * (n_cw + n_hw),
        out_specs=[hbm] * (2 + n_hw),
        out_shape=[jax.ShapeDtypeStruct(x.shape, x.dtype), jax.ShapeDtypeStruct(x.shape, _BF16)]
        + [jax.ShapeDtypeStruct(w.shape[1:], _BF16) for w, _ in handoff_w],
        scratch_shapes=[pltpu.VMEM(w.shape[1:], _BF16) for w, _ in channel_w],
        compiler_params=pltpu.CompilerParams(vmem_limit_bytes=VMEM_LIMIT_BYTES),
        name=f"layer_{layer}",
    )(x, p, mixer_w_hbm, *mixer_small, *ln, *[w for w, _ in casts])
    return outs[0], outs[2:]


def kernel(x, p, ev_w_in, ev_pool_w, ev_pool_scale, ev_sgu_ln_g, ev_sgu_ln_b, ev_sgu_w, ev_sgu_b,
           ev_w_out, od_w_in, od_conv_w, od_w_out, ln1_g, ln1_b, ffn_w1, ffn_w2, ln2_g, ln2_b,
           ple_gate_w, ple_w):
    assert x.shape[1] % SEQ_TILE == 0 and SEQ_TILE % SUB_ROWS == 0 and SUB_ROWS % CHUNK == 0
    assert ln1_g.shape[0] == DEPTH and ev_w_in.shape[0] == 1 and od_w_in.shape[0] == 1
    bsz, _, d = x.shape
    ln = (ln1_g, ln1_b, ln2_g, ln2_b)
    channel_w = lambda w_out, i: [(w_out, 0), (ffn_w1, i), (ffn_w2, i), (ple_gate_w, i),
                                  (ple_w, i)]

    def even_prep(w_hbm, small, scratch):
        pool_w_ref, _, _, _, sgu_w_ref, sgu_pos_b_ref = small
        w_in_ref, pool_bd_ref, w_tri_ref, sgu_bias_ref, ext_ref, lvl2_ref, lvl4_ref, _ = scratch

        def with_f32_weight(w_f32_ref, sem):
            load = pltpu.make_async_copy(w_hbm.at[0], w_f32_ref, sem)
            load.start()
            load.wait()
            _even_mixer_prep(w_f32_ref, pool_w_ref.at[0], sgu_w_ref.at[0], sgu_pos_b_ref.at[0],
                             w_in_ref, pool_bd_ref, w_tri_ref, sgu_bias_ref,
                             ext_ref, lvl2_ref, lvl4_ref)

        pl.run_scoped(with_f32_weight, pltpu.VMEM(ev_w_in.shape[1:], _F32),
                      pltpu.SemaphoreType.DMA(()))

    def even_step(s, x_ref, small, y_ref, scratch):
        _, pool_scale_ref, sgu_g_ref, sgu_b_ref, _, _ = small
        _even_mixer_kernel(s, x_ref, pool_scale_ref, sgu_g_ref, sgu_b_ref, y_ref, *scratch)

    pool_buf = pltpu.VMEM((bsz, POOL_BASE + SEQ_TILE, POOL_WIDTH), _F32)
    even_scratch = [pltpu.VMEM(ev_w_in.shape[1:], _BF16),
                    pltpu.VMEM((2, 2 * POOL_GROUP_DIM, 2 * POOL_GROUP_DIM), _BF16),
                    pltpu.VMEM((SGU_HEADS, CHUNK, CHUNK), _BF16),
                    pltpu.VMEM((CHUNK, SGU_WIDTH), _F32),
                    pool_buf, pool_buf, pool_buf, pool_buf]
    x, (w_in,) = _layer(0, x, p, ln, ev_w_in,
                        (ev_pool_w, ev_pool_scale, ev_sgu_ln_g, ev_sgu_ln_b, ev_sgu_w, ev_sgu_b),
                        even_scratch, even_prep, even_step,
                        channel_w(ev_w_out, 0), [(od_w_in, 0)])

    def odd_prep(w_hbm, small, scratch):
        w_in_ref = scratch[0]

        def with_sem(sem):
            load = pltpu.make_async_copy(w_hbm, w_in_ref, sem)
            load.start()
            load.wait()

        pl.run_scoped(with_sem, pltpu.SemaphoreType.DMA(()))

    def odd_step(s, x_ref, small, y_ref, scratch):
        w_in_ref, z_ref = scratch
        _odd_mixer_kernel(s, x_ref, w_in_ref, small[0].at[0], y_ref, z_ref)

    odd_scratch = [pltpu.VMEM(w_in.shape, _BF16),
                   pltpu.VMEM((bsz, CONV_HALO + SEQ_TILE, od_conv_w.shape[-1]), _F32)]
    x, _ = _layer(1, x, p, ln, w_in, (od_conv_w,), odd_scratch, odd_prep, odd_step,
                  channel_w(od_w_out, 1), [])
    return x
```

```python
import jax
import jax.numpy as jnp
from jax import lax
from jax.experimental import pallas as pl
from jax.experimental.pallas import tpu as pltpu

POOL_WINDOWS = (2, 4, 8, 16)
POOL_GROUP_DIM = 128
POOL_WIDTH = POOL_GROUP_DIM * len(POOL_WINDOWS)
SGU_HEADS = 4
SGU_HEAD_DIM = 128
SGU_WIDTH = SGU_HEADS * SGU_HEAD_DIM
CHUNK = 128
CONV_WIDTH = 3
LN_EPS = 1e-5
DEPTH = 2
DEEPNORM_ALPHA = (2.0 * DEPTH) ** 0.25

SUBLANES = 8
POOL_HALO = max(POOL_WINDOWS)
POOL_PAD = SUBLANES
POOL_BASE = POOL_PAD + POOL_HALO
CONV_HALO = SUBLANES

SEQ_TILE = 512
SUB_ROWS = 256
LHS_SLOTS = 2
VMEM_LIMIT_BYTES = 60 * 1024 * 1024

_BF16 = jnp.bfloat16
_F32 = jnp.float32


def _dot(a, b):
    return jnp.dot(a, b, preferred_element_type=_F32)


def _layer_norm(x, g, b):
    mu = jnp.mean(x, axis=-1, keepdims=True)
    xc = x - mu
    var = jnp.mean(xc * xc, axis=-1, keepdims=True)
    return xc * lax.rsqrt(var + LN_EPS) * g + b


def _chains(x_ref):
    bsz, ts, _ = x_ref.shape
    return [(b, r) for b in range(bsz) for r in range(0, ts, SUB_ROWS)]


def _layer_row(ref, layer):
    return ref[layer:layer + 1, :]


def _even_mixer_prep(w_in_f32_ref, pool_w_ref, sgu_w_ref, sgu_pos_b_ref,
                     w_in_ref, pool_bd_ref, w_tri_ref, sgu_bias_ref, ext_ref, lvl2_ref, lvl4_ref):
    g, d, base = POOL_GROUP_DIM, SGU_HEAD_DIM, POOL_BASE
    w_in_ref[...] = w_in_f32_ref[...].astype(_BF16)
    pool_bd_ref[...] = jnp.zeros(pool_bd_ref.shape, _BF16)
    for k in range(len(POOL_WINDOWS)):
        off = (k % 2) * g
        pool_bd_ref[k // 2, off:off + g, off:off + g] = pool_w_ref[k].astype(_BF16)
    tri = (lax.broadcasted_iota(jnp.int32, (CHUNK, CHUNK), 0)
           >= lax.broadcasted_iota(jnp.int32, (CHUNK, CHUNK), 1))
    for h in range(SGU_HEADS):
        w_tri_ref[h] = jnp.where(tri, sgu_w_ref[h], 0.0).astype(_BF16)
        pos_b = jnp.broadcast_to(sgu_pos_b_ref[h:h + 1, :], (SGU_HEAD_DIM, CHUNK))
        sgu_bias_ref[:, h * d:(h + 1) * d] = pos_b.T
    for b in range(ext_ref.shape[0]):
        ext_ref[b, 0:base, :] = jnp.zeros((base, POOL_WIDTH), _F32)
        lvl2_ref[b, 0:POOL_PAD, :] = jnp.zeros((POOL_PAD, POOL_WIDTH), _F32)
        lvl4_ref[b, 0:POOL_PAD, :] = jnp.zeros((POOL_PAD, POOL_WIDTH), _F32)


def _even_mixer_kernel(s, x_ref, pool_scale_ref, sgu_g_ref, sgu_b_ref, y_ref,
                       w_in_ref, pool_bd_ref, w_tri_ref, sgu_bias_ref,
                       ext_ref, lvl2_ref, lvl4_ref, lvl8_ref):
    bsz, ts, _ = x_ref.shape
    sub, g, d, base = SUB_ROWS, POOL_GROUP_DIM, SGU_HEAD_DIM, POOL_BASE
    chains = _chains(x_ref)

    a, u, v, pooled, vn = {}, {}, {}, {}, {}
    for ch in chains:
        b, r = ch
        xb = x_ref[b, r:r + sub, :].astype(_BF16)
        a[ch] = _dot(xb, w_in_ref[:, 0:POOL_WIDTH])
        ext_ref[b, base + r:base + r + sub, :] = a[ch]
        u[ch] = _dot(xb, w_in_ref[:, POOL_WIDTH:POOL_WIDTH + SGU_WIDTH])
        v[ch] = _dot(xb, w_in_ref[:, POOL_WIDTH + SGU_WIDTH:])

    for ch in chains:
        b, r = ch
        lo = POOL_PAD if r == 0 else base + r
        hi = base + r + sub
        own = slice(base + r, hi)
        lvl2_ref[b, lo:hi, :] = ext_ref[b, lo:hi, :] + ext_ref[b, lo - 1:hi - 1, :]
        lvl4_ref[b, lo:hi, g:] = lvl2_ref[b, lo:hi, g:] + lvl2_ref[b, lo - 2:hi - 2, g:]
        lvl8_ref[b, lo:hi, 2 * g:] = lvl4_ref[b, lo:hi, 2 * g:] + lvl4_ref[b, lo - 4:hi - 4, 2 * g:]
        sums = (
            lvl2_ref[b, own, 0:g],
            lvl4_ref[b, own, g:2 * g],
            lvl8_ref[b, own, 2 * g:3 * g],
            lvl8_ref[b, own, 3 * g:4 * g] + lvl8_ref[b, base + r - 8:hi - 8, 3 * g:4 * g],
        )
        t1 = lax.broadcasted_iota(jnp.int32, (sub, g), 0) + (s * ts + r + 1)
        parts = []
        for gi, w in enumerate(POOL_WINDOWS):
            cnt = jnp.minimum(t1, w).astype(_F32)
            parts.append((sums[gi] / cnt - a[ch][:, gi * g:(gi + 1) * g]).astype(_BF16))
        pooled[ch] = parts
        vn[ch] = _layer_norm(v[ch], sgu_g_ref[...], sgu_b_ref[...]).astype(_BF16)

    for b in range(bsz):
        ext_ref[b, POOL_PAD:base, :] = a[(b, ts - sub)][sub - POOL_HALO:sub, :]

    scale = pool_scale_ref[...]
    n_chunks = sub // CHUNK
    for ch in chains:
        b, r = ch
        for pair in range(2):
            pb = jnp.concatenate(pooled[ch][2 * pair:2 * pair + 2], axis=1)
            ya = _dot(pb, pool_bd_ref[pair]) * scale[:, 2 * g * pair:2 * g * (pair + 1)]
            y_ref[b, r:r + sub, 2 * g * pair:2 * g * (pair + 1)] = ya.astype(_BF16)
        for h in range(SGU_HEADS):
            rhs = jnp.concatenate(
                [vn[ch][c * CHUNK:(c + 1) * CHUNK, h * d:(h + 1) * d] for c in range(n_chunks)],
                axis=1)
            mixed = _dot(w_tri_ref[h], rhs)
            bias_h = sgu_bias_ref[:, h * d:(h + 1) * d]
            for c in range(n_chunks):
                m = mixed[:, c * d:(c + 1) * d] + bias_h
                yb = u[ch][c * CHUNK:(c + 1) * CHUNK, h * d:(h + 1) * d] * m
                y_ref[b, r + c * CHUNK:r + (c + 1) * CHUNK,
                      POOL_WIDTH + h * d:POOL_WIDTH + (h + 1) * d] = yb.astype(_BF16)


def _odd_mixer_kernel(s, x_ref, w_in_ref, conv_w_ref, y_ref, z_ref):
    bsz, ts, _ = x_ref.shape
    sub = SUB_ROWS
    cd = conv_w_ref.shape[-1]
    chains = _chains(x_ref)

    @pl.when(s == 0)
    def _():
        for b in range(bsz):
            z_ref[b, 0:CONV_HALO, :] = jnp.zeros((CONV_HALO, cd), _F32)

    z, gate_b = {}, {}
    for ch in chains:
        b, r = ch
        xb = x_ref[b, r:r + sub, :].astype(_BF16)
        gate_c = _dot(xb, w_in_ref[:, cd:2 * cd])
        hid = _dot(xb, w_in_ref[:, 2 * cd:3 * cd])
        z[ch] = gate_c * hid
        z_ref[b, CONV_HALO + r:CONV_HALO + r + sub, :] = z[ch]
        gate_b[ch] = _dot(xb, w_in_ref[:, 0:cd])
    cw = conv_w_ref[...]
    for ch in chains:
        b, r = ch
        conv = cw[CONV_WIDTH - 1:CONV_WIDTH, :] * z[ch]
        for k in range(CONV_WIDTH - 1):
            off = CONV_HALO + r - (CONV_WIDTH - 1) + k
            conv = conv + cw[k:k + 1, :] * z_ref[b, off:off + sub, :]
        y_ref[b, r:r + sub, :] = (gate_b[ch] * conv).astype(_BF16)
    for b in range(bsz):
        z_ref[b, 0:CONV_HALO, :] = z[(b, ts - sub)][sub - CONV_HALO:sub, :]


def _channel_kernel(x_ref, y_ref, p_ref, w_out_ref, ln1_g_ref, ln1_b_ref, w1_ref, w2_ref,
                    ln2_g_ref, ln2_b_ref, wg_ref, wp_ref, o_ref, xb_ref, hid_ref, x2b_ref,
                    *, layer):
    sub = SUB_ROWS
    gain1, bias1 = _layer_row(ln1_g_ref, layer), _layer_row(ln1_b_ref, layer)
    gain2, bias2 = _layer_row(ln2_g_ref, layer), _layer_row(ln2_b_ref, layer)
    d_ff = w1_ref.shape[1]
    ff_chunk = w1_ref.shape[0]
    chains = _chains(x_ref)

    def out_proj(ch):
        b, r = ch
        return _dot(y_ref[b, r:r + sub, :], w_out_ref[...])

    def mlp(ch, slot, mix):
        b, r = ch
        x1 = _layer_norm(DEEPNORM_ALPHA * x_ref[b, r:r + sub, :] + mix, gain1, bias1)
        xb_ref[slot] = x1.astype(_BF16)
        for c in range(d_ff // ff_chunk):
            h = _dot(xb_ref[slot], w1_ref[:, c * ff_chunk:(c + 1) * ff_chunk])
            h = jnp.maximum(h, 0.0)
            hid_ref[slot, :, c * ff_chunk:(c + 1) * ff_chunk] = (h * h).astype(_BF16)
        acc = _dot(hid_ref[slot], w2_ref[...])
        return _layer_norm(DEEPNORM_ALPHA * x1 + acc, gain2, bias2)

    def embed_gate(ch, slot, x2):
        b, r = ch
        x2b_ref[slot] = x2.astype(_BF16)
        gate = jax.nn.sigmoid(_dot(x2b_ref[slot], wg_ref[...]))
        pp = _dot(p_ref[b, r:r + sub, :].astype(_BF16), wp_ref[...])
        o_ref[b, r:r + sub, :] = x2 + gate * pp

    mix = out_proj(chains[0])
    pending = None
    for i, ch in enumerate(chains):
        next_mix = out_proj(chains[i + 1]) if i + 1 < len(chains) else None
        x2 = mlp(ch, i % LHS_SLOTS, mix)
        if pending is not None:
            embed_gate(*pending)
        pending = (ch, i % LHS_SLOTS, x2)
        mix = next_mix
    embed_gate(*pending)


def _layer(layer, x, p, ln, mixer_w_hbm, mixer_small, mixer_scratch, mixer_prep, mixer_step,
           channel_w, handoff_w):
    bsz, seq, d = x.shape
    steps = seq // SEQ_TILE
    act = pl.BlockSpec((bsz, SEQ_TILE, d), lambda s: (0, s, 0))
    p_in = pl.BlockSpec((bsz, SEQ_TILE, p.shape[-1]), lambda s: (0, s, 0))
    casts = list(channel_w) + list(handoff_w)
    slabs = [w.shape[1] // steps for w, _ in casts]
    assert all(slab * steps == w.shape[1] and slab % (2 * SUBLANES) == 0
               for slab, (w, _) in zip(slabs, casts))
    cast_specs = [pl.BlockSpec((slab, w.shape[2]), lambda s: (s, 0))
                  for slab, (w, _) in zip(slabs, casts)]
    n_small, n_cw, n_hw = len(mixer_small), len(channel_w), len(handoff_w)
    d_ff = channel_w[1][0].shape[-1]
    channel_scratch = [pltpu.VMEM((LHS_SLOTS, SUB_ROWS, d), _BF16),
                       pltpu.VMEM((LHS_SLOTS, SUB_ROWS, d_ff), _BF16),
                       pltpu.VMEM((LHS_SLOTS, SUB_ROWS, d), _BF16)]

    def body(x_hbm, p_hbm, w_hbm, *refs):
        small = refs[:n_small]
        ln_refs = refs[n_small:n_small + 4]
        srcs = refs[n_small + 4:n_small + 4 + n_cw + n_hw]
        k = n_small + 4 + n_cw + n_hw
        o_hbm, y_hbm = refs[k:k + 2]
        handoff = refs[k + 2:k + 2 + n_hw]
        w_refs = refs[k + 2 + n_hw:]

        def mixer_phase(tile_ref, *scratch):
            finish_prep = mixer_prep(w_hbm, small, scratch)
            tile_ref[0] = 0

            def step(x_ref, *tiles):
                src_tiles = tiles[:n_cw + n_hw]
                y_ref = tiles[n_cw + n_hw]
                out_tiles = tiles[n_cw + n_hw + 1:]
                s = tile_ref[0]
                if finish_prep is not None:
                    pl.when(s == 0)(finish_prep)
                mixer_step(s, x_ref, small, y_ref, scratch)
                for src, dst, slab in zip(src_tiles[:n_cw], w_refs, slabs):
                    row = pl.multiple_of(s * slab, slab)
                    dst[pl.ds(row, slab), :] = src[...].astype(_BF16)
                for src, dst in zip(src_tiles[n_cw:], out_tiles):
                    dst[...] = src[...].astype(_BF16)
                tile_ref[0] = s + 1

            pltpu.emit_pipeline(step, grid=(steps,), in_specs=[act] + cast_specs,
                                out_specs=[act] + cast_specs[n_cw:])(
                x_hbm, *[src.at[idx] for src, (_, idx) in zip(srcs, casts)], y_hbm, *handoff)

        pl.run_scoped(mixer_phase, pltpu.SMEM((1,), jnp.int32), *mixer_scratch)

        def channel_phase(xb_ref, hid_ref, x2b_ref):
            w_out_ref, w1_ref, w2_ref, wg_ref, wp_ref = w_refs
            g1_ref, b1_ref, g2_ref, b2_ref = ln_refs

            def step(x_ref, y_ref, p_ref, o_ref):
                _channel_kernel(x_ref, y_ref, p_ref, w_out_ref, g1_ref, b1_ref, w1_ref, w2_ref,
                                g2_ref, b2_ref, wg_ref, wp_ref, o_ref, xb_ref, hid_ref, x2b_ref,
                                layer=layer)

            pltpu.emit_pipeline(step, grid=(steps,), in_specs=[act, act, p_in], out_specs=[act])(
                x_hbm, y_hbm, p_hbm.at[layer], o_hbm)

        pl.run_scoped(channel_phase, *channel_scratch)

    hbm = pl.BlockSpec(memory_space=pl.ANY)
    vmem = pl.BlockSpec(memory_space=pltpu.VMEM)
    outs = pl.pallas_call(
        body,
        in_specs=[hbm, hbm, hbm] + [vmem] * (n_small + 4) + [hbm] * (n_cw + n_hw),
        out_specs=[hbm] * (2 + n_hw),
        out_shape=[jax.ShapeDtypeStruct(x.shape, x.dtype), jax.ShapeDtypeStruct(x.shape, _BF16)]
        + [jax.ShapeDtypeStruct(w.shape[1:], _BF16) for w, _ in handoff_w],
        scratch_shapes=[pltpu.VMEM(w.shape[1:], _BF16) for w, _ in channel_w],
        compiler_params=pltpu.CompilerParams(vmem_limit_bytes=VMEM_LIMIT_BYTES),
        name=f"layer_{layer}",
    )(x, p, mixer_w_hbm, *mixer_small, *ln, *[w for w, _ in casts])
    return outs[0], outs[2:]


def kernel(x, p, ev_w_in, ev_pool_w, ev_pool_scale, ev_sgu_ln_g, ev_sgu_ln_b, ev_sgu_w, ev_sgu_b,
           ev_w_out, od_w_in, od_conv_w, od_w_out, ln1_g, ln1_b, ffn_w1, ffn_w2, ln2_g, ln2_b,
           ple_gate_w, ple_w):
    assert x.shape[1] % SEQ_TILE == 0 and SEQ_TILE % SUB_ROWS == 0 and SUB_ROWS % CHUNK == 0
    assert ln1_g.shape[0] == DEPTH and ev_w_in.shape[0] == 1 and od_w_in.shape[0] == 1
    bsz, _, d = x.shape
    ln = (ln1_g, ln1_b, ln2_g, ln2_b)
    channel_w = lambda w_out, i: [(w_out, 0), (ffn_w1, i), (ffn_w2, i), (ple_gate_w, i),
                                  (ple_w, i)]

    def even_prep(w_hbm, small, scratch):
        pool_w_ref, _, _, _, sgu_w_ref, sgu_pos_b_ref = small
        w_in_ref, pool_bd_ref, w_tri_ref, sgu_bias_ref, ext_ref, lvl2_ref, lvl4_ref, _ = scratch

        def with_f32_weight(w_f32_ref, sem):
            load = pltpu.make_async_copy(w_hbm.at[0], w_f32_ref, sem)
            load.start()
            load.wait()
            _even_mixer_prep(w_f32_ref, pool_w_ref.at[0], sgu_w_ref.at[0], sgu_pos_b_ref.at[0],
                             w_in_ref, pool_bd_ref, w_tri_ref, sgu_bias_ref,
                             ext_ref, lvl2_ref, lvl4_ref)

        pl.run_scoped(with_f32_weight, pltpu.VMEM(ev_w_in.shape[1:], _F32),
                      pltpu.SemaphoreType.DMA(()))

    def even_step(s, x_ref, small, y_ref, scratch):
        _, pool_scale_ref, sgu_g_ref, sgu_b_ref, _, _ = small
        _even_mixer_kernel(s, x_ref, pool_scale_ref, sgu_g_ref, sgu_b_ref, y_ref, *scratch)

    pool_buf = pltpu.VMEM((bsz, POOL_BASE + SEQ_TILE, POOL_WIDTH), _F32)
    even_scratch = [pltpu.VMEM(ev_w_in.shape[1:], _BF16),
                    pltpu.VMEM((2, 2 * POOL_GROUP_DIM, 2 * POOL_GROUP_DIM), _BF16),
                    pltpu.VMEM((SGU_HEADS, CHUNK, CHUNK), _BF16),
                    pltpu.VMEM((CHUNK, SGU_WIDTH), _F32),
                    pool_buf, pool_buf, pool_buf, pool_buf]
    x, (w_in,) = _layer(0, x, p, ln, ev_w_in,
                        (ev_pool_w, ev_pool_scale, ev_sgu_ln_g, ev_sgu_ln_b, ev_sgu_w, ev_sgu_b),
                        even_scratch, even_prep, even_step,
                        channel_w(ev_w_out, 0), [(od_w_in, 0)])

    def odd_prep(w_hbm, small, scratch):
        w_in_ref, _, sem = scratch
        load = pltpu.make_async_copy(w_hbm, w_in_ref, sem)
        load.start()
        return load.wait

    def odd_step(s, x_ref, small, y_ref, scratch):
        w_in_ref, z_ref, _ = scratch
        _odd_mixer_kernel(s, x_ref, w_in_ref, small[0].at[0], y_ref, z_ref)

    odd_scratch = [pltpu.VMEM(w_in.shape, _BF16),
                   pltpu.VMEM((bsz, CONV_HALO + SEQ_TILE, od_conv_w.shape[-1]), _F32),
                   pltpu.SemaphoreType.DMA(())]
    x, _ = _layer(1, x, p, ln, w_in, (od_conv_w,), odd_scratch, odd_prep, odd_step,
                  channel_w(od_w_out, 1), [])
    return x
```

```python
import jax
import jax.numpy as jnp
from jax import lax
from jax.experimental import pallas as pl
from jax.experimental.pallas import tpu as pltpu

POOL_WINDOWS = (2, 4, 8, 16)
POOL_GROUP_DIM = 128
POOL_WIDTH = POOL_GROUP_DIM * len(POOL_WINDOWS)
SGU_HEADS = 4
SGU_HEAD_DIM = 128
SGU_WIDTH = SGU_HEADS * SGU_HEAD_DIM
CHUNK = 128
CONV_WIDTH = 3
LN_EPS = 1e-5
DEPTH = 2
DEEPNORM_ALPHA = (2.0 * DEPTH) ** 0.25

SUBLANES = 8
POOL_HALO = max(POOL_WINDOWS)
POOL_PAD = SUBLANES
POOL_BASE = POOL_PAD + POOL_HALO
CONV_HALO = SUBLANES

SEQ_TILE = 512
SUB_ROWS = 256
LHS_SLOTS = 2
VMEM_LIMIT_BYTES = 60 * 1024 * 1024

_BF16 = jnp.bfloat16
_F32 = jnp.float32


def _dot(a, b):
    return jnp.dot(a, b, preferred_element_type=_F32)


def _layer_norm(x, g, b):
    mu = jnp.mean(x, axis=-1, keepdims=True)
    xc = x - mu
    var = jnp.mean(xc * xc, axis=-1, keepdims=True)
    return xc * lax.rsqrt(var + LN_EPS) * g + b


def _chains(x_ref):
    bsz, ts, _ = x_ref.shape
    return [(b, r) for b in range(bsz) for r in range(0, ts, SUB_ROWS)]


def _layer_row(ref, layer):
    return ref[layer:layer + 1, :]


def _even_mixer_prep(pool_w_ref, sgu_w_ref, sgu_pos_b_ref,
                     pool_bd_ref, w_tri_ref, sgu_bias_ref, ext_ref, lvl2_ref, lvl4_ref):
    g, d, base = POOL_GROUP_DIM, SGU_HEAD_DIM, POOL_BASE
    pool_bd_ref[...] = jnp.zeros(pool_bd_ref.shape, _BF16)
    for k in range(len(POOL_WINDOWS)):
        off = (k % 2) * g
        pool_bd_ref[k // 2, off:off + g, off:off + g] = pool_w_ref[k].astype(_BF16)
    tri = (lax.broadcasted_iota(jnp.int32, (CHUNK, CHUNK), 0)
           >= lax.broadcasted_iota(jnp.int32, (CHUNK, CHUNK), 1))
    for h in range(SGU_HEADS):
        w_tri_ref[h] = jnp.where(tri, sgu_w_ref[h], 0.0).astype(_BF16)
        pos_b = jnp.broadcast_to(sgu_pos_b_ref[h:h + 1, :], (SGU_HEAD_DIM, CHUNK))
        sgu_bias_ref[:, h * d:(h + 1) * d] = pos_b.T
    for b in range(ext_ref.shape[0]):
        ext_ref[b, 0:base, :] = jnp.zeros((base, POOL_WIDTH), _F32)
        lvl2_ref[b, 0:POOL_PAD, :] = jnp.zeros((POOL_PAD, POOL_WIDTH), _F32)
        lvl4_ref[b, 0:POOL_PAD, :] = jnp.zeros((POOL_PAD, POOL_WIDTH), _F32)


def _even_mixer_kernel(s, x_ref, pool_scale_ref, sgu_g_ref, sgu_b_ref, y_ref,
                       w_in_ref, pool_bd_ref, w_tri_ref, sgu_bias_ref,
                       ext_ref, lvl2_ref, lvl4_ref, lvl8_ref):
    bsz, ts, _ = x_ref.shape
    sub, g, d, base = SUB_ROWS, POOL_GROUP_DIM, SGU_HEAD_DIM, POOL_BASE
    chains = _chains(x_ref)

    a, u, v, pooled, vn = {}, {}, {}, {}, {}
    for ch in chains:
        b, r = ch
        xb = x_ref[b, r:r + sub, :].astype(_BF16)
        a[ch] = _dot(xb, w_in_ref[:, 0:POOL_WIDTH])
        ext_ref[b, base + r:base + r + sub, :] = a[ch]
        u[ch] = _dot(xb, w_in_ref[:, POOL_WIDTH:POOL_WIDTH + SGU_WIDTH])
        v[ch] = _dot(xb, w_in_ref[:, POOL_WIDTH + SGU_WIDTH:])

    for ch in chains:
        b, r = ch
        lo = POOL_PAD if r == 0 else base + r
        hi = base + r + sub
        own = slice(base + r, hi)
        lvl2_ref[b, lo:hi, :] = ext_ref[b, lo:hi, :] + ext_ref[b, lo - 1:hi - 1, :]
        lvl4_ref[b, lo:hi, g:] = lvl2_ref[b, lo:hi, g:] + lvl2_ref[b, lo - 2:hi - 2, g:]
        lvl8_ref[b, lo:hi, 2 * g:] = lvl4_ref[b, lo:hi, 2 * g:] + lvl4_ref[b, lo - 4:hi - 4, 2 * g:]
        sums = (
            lvl2_ref[b, own, 0:g],
            lvl4_ref[b, own, g:2 * g],
            lvl8_ref[b, own, 2 * g:3 * g],
            lvl8_ref[b, own, 3 * g:4 * g] + lvl8_ref[b, base + r - 8:hi - 8, 3 * g:4 * g],
        )
        t1 = lax.broadcasted_iota(jnp.int32, (sub, g), 0) + (s * ts + r + 1)
        parts = []
        for gi, w in enumerate(POOL_WINDOWS):
            cnt = jnp.minimum(t1, w).astype(_F32)
            parts.append((sums[gi] / cnt - a[ch][:, gi * g:(gi + 1) * g]).astype(_BF16))
        pooled[ch] = parts
        vn[ch] = _layer_norm(v[ch], sgu_g_ref[...], sgu_b_ref[...]).astype(_BF16)

    for b in range(bsz):
        ext_ref[b, POOL_PAD:base, :] = a[(b, ts - sub)][sub - POOL_HALO:sub, :]

    scale = pool_scale_ref[...]
    n_chunks = sub // CHUNK
    for ch in chains:
        b, r = ch
        for pair in range(2):
            pb = jnp.concatenate(pooled[ch][2 * pair:2 * pair + 2], axis=1)
            ya = _dot(pb, pool_bd_ref[pair]) * scale[:, 2 * g * pair:2 * g * (pair + 1)]
            y_ref[b, r:r + sub, 2 * g * pair:2 * g * (pair + 1)] = ya.astype(_BF16)
        for h in range(SGU_HEADS):
            rhs = jnp.concatenate(
                [vn[ch][c * CHUNK:(c + 1) * CHUNK, h * d:(h + 1) * d] for c in range(n_chunks)],
                axis=1)
            mixed = _dot(w_tri_ref[h], rhs)
            bias_h = sgu_bias_ref[:, h * d:(h + 1) * d]
            for c in range(n_chunks):
                m = mixed[:, c * d:(c + 1) * d] + bias_h
                yb = u[ch][c * CHUNK:(c + 1) * CHUNK, h * d:(h + 1) * d] * m
                y_ref[b, r + c * CHUNK:r + (c + 1) * CHUNK,
                      POOL_WIDTH + h * d:POOL_WIDTH + (h + 1) * d] = yb.astype(_BF16)


def _odd_mixer_kernel(s, x_ref, w_in_ref, conv_w_ref, y_ref, z_ref):
    bsz, ts, _ = x_ref.shape
    sub = SUB_ROWS
    cd = conv_w_ref.shape[-1]
    chains = _chains(x_ref)

    @pl.when(s == 0)
    def _():
        for b in range(bsz):
            z_ref[b, 0:CONV_HALO, :] = jnp.zeros((CONV_HALO, cd), _F32)

    z, gate_b = {}, {}
    for ch in chains:
        b, r = ch
        xb = x_ref[b, r:r + sub, :].astype(_BF16)
        gate_c = _dot(xb, w_in_ref[:, cd:2 * cd])
        hid = _dot(xb, w_in_ref[:, 2 * cd:3 * cd])
        z[ch] = gate_c * hid
        z_ref[b, CONV_HALO + r:CONV_HALO + r + sub, :] = z[ch]
        gate_b[ch] = _dot(xb, w_in_ref[:, 0:cd])
    cw = conv_w_ref[...]
    for ch in chains:
        b, r = ch
        conv = cw[CONV_WIDTH - 1:CONV_WIDTH, :] * z[ch]
        for k in range(CONV_WIDTH - 1):
            off = CONV_HALO + r - (CONV_WIDTH - 1) + k
            conv = conv + cw[k:k + 1, :] * z_ref[b, off:off + sub, :]
        y_ref[b, r:r + sub, :] = (gate_b[ch] * conv).astype(_BF16)
    for b in range(bsz):
        z_ref[b, 0:CONV_HALO, :] = z[(b, ts - sub)][sub - CONV_HALO:sub, :]


def _channel_kernel(x_ref, y_ref, p_ref, w_out_ref, ln1_g_ref, ln1_b_ref, w1_ref, w2_ref,
                    ln2_g_ref, ln2_b_ref, wg_ref, wp_ref, o_ref, xb_ref, hid_ref, x2b_ref,
                    *, layer):
    sub = SUB_ROWS
    gain1, bias1 = _layer_row(ln1_g_ref, layer), _layer_row(ln1_b_ref, layer)
    gain2, bias2 = _layer_row(ln2_g_ref, layer), _layer_row(ln2_b_ref, layer)
    d_ff = w1_ref.shape[1]
    ff_chunk = w1_ref.shape[0]
    chains = _chains(x_ref)

    def out_proj(ch):
        b, r = ch
        return _dot(y_ref[b, r:r + sub, :], w_out_ref[...])

    def mlp(ch, slot, mix):
        b, r = ch
        x1 = _layer_norm(DEEPNORM_ALPHA * x_ref[b, r:r + sub, :] + mix, gain1, bias1)
        xb_ref[slot] = x1.astype(_BF16)
        for c in range(d_ff // ff_chunk):
            h = _dot(xb_ref[slot], w1_ref[:, c * ff_chunk:(c + 1) * ff_chunk])
            h = jnp.maximum(h, 0.0)
            hid_ref[slot, :, c * ff_chunk:(c + 1) * ff_chunk] = (h * h).astype(_BF16)
        acc = _dot(hid_ref[slot], w2_ref[...])
        return _layer_norm(DEEPNORM_ALPHA * x1 + acc, gain2, bias2)

    def embed_gate(ch, slot, x2):
        b, r = ch
        x2b_ref[slot] = x2.astype(_BF16)
        gate = jax.nn.sigmoid(_dot(x2b_ref[slot], wg_ref[...]))
        pp = _dot(p_ref[b, r:r + sub, :].astype(_BF16), wp_ref[...])
        o_ref[b, r:r + sub, :] = x2 + gate * pp

    mix = out_proj(chains[0])
    pending = None
    for i, ch in enumerate(chains):
        next_mix = out_proj(chains[i + 1]) if i + 1 < len(chains) else None
        x2 = mlp(ch, i % LHS_SLOTS, mix)
        if pending is not None:
            embed_gate(*pending)
        pending = (ch, i % LHS_SLOTS, x2)
        mix = next_mix
    embed_gate(*pending)


def _layer(layer, x, p, ln, mixer_w_hbm, mixer_small, mixer_scratch, mixer_prep, mixer_step,
           channel_w, handoff_w):
    bsz, seq, d = x.shape
    steps = seq // SEQ_TILE
    act = pl.BlockSpec((bsz, SEQ_TILE, d), lambda s: (0, s, 0))
    p_in = pl.BlockSpec((bsz, SEQ_TILE, p.shape[-1]), lambda s: (0, s, 0))
    casts = list(channel_w) + list(handoff_w)
    slabs = [w.shape[1] // steps for w, _ in casts]
    assert all(slab * steps == w.shape[1] and slab % (2 * SUBLANES) == 0
               for slab, (w, _) in zip(slabs, casts))
    cast_specs = [pl.BlockSpec((slab, w.shape[2]), lambda s: (s, 0))
                  for slab, (w, _) in zip(slabs, casts)]
    n_small, n_cw, n_hw = len(mixer_small), len(channel_w), len(handoff_w)
    d_ff = channel_w[1][0].shape[-1]
    channel_scratch = [pltpu.VMEM((LHS_SLOTS, SUB_ROWS, d), _BF16),
                       pltpu.VMEM((LHS_SLOTS, SUB_ROWS, d_ff), _BF16),
                       pltpu.VMEM((LHS_SLOTS, SUB_ROWS, d), _BF16)]

    def body(x_hbm, p_hbm, w_hbm, *refs):
        small = refs[:n_small]
        ln_refs = refs[n_small:n_small + 4]
        srcs = refs[n_small + 4:n_small + 4 + n_cw + n_hw]
        k = n_small + 4 + n_cw + n_hw
        o_hbm, y_hbm = refs[k:k + 2]
        handoff = refs[k + 2:k + 2 + n_hw]
        w_refs = refs[k + 2 + n_hw:]

        def mixer_phase(tile_ref, *scratch):
            finish_prep = mixer_prep(w_hbm, small, scratch)
            tile_ref[0] = 0

            def step(x_ref, *tiles):
                src_tiles = tiles[:n_cw + n_hw]
                y_ref = tiles[n_cw + n_hw]
                out_tiles = tiles[n_cw + n_hw + 1:]
                s = tile_ref[0]
                if finish_prep is not None:
                    pl.when(s == 0)(finish_prep)
                mixer_step(s, x_ref, small, y_ref, scratch)
                for src, dst, slab in zip(src_tiles[:n_cw], w_refs, slabs):
                    row = pl.multiple_of(s * slab, slab)
                    dst[pl.ds(row, slab), :] = src[...].astype(_BF16)
                for src, dst in zip(src_tiles[n_cw:], out_tiles):
                    dst[...] = src[...].astype(_BF16)
                tile_ref[0] = s + 1

            pltpu.emit_pipeline(step, grid=(steps,), in_specs=[act] + cast_specs,
                                out_specs=[act] + cast_specs[n_cw:])(
                x_hbm, *[src.at[idx] for src, (_, idx) in zip(srcs, casts)], y_hbm, *handoff)

        pl.run_scoped(mixer_phase, pltpu.SMEM((1,), jnp.int32), *mixer_scratch)

        def channel_phase(xb_ref, hid_ref, x2b_ref):
            w_out_ref, w1_ref, w2_ref, wg_ref, wp_ref = w_refs
            g1_ref, b1_ref, g2_ref, b2_ref = ln_refs

            def step(x_ref, y_ref, p_ref, o_ref):
                _channel_kernel(x_ref, y_ref, p_ref, w_out_ref, g1_ref, b1_ref, w1_ref, w2_ref,
                                g2_ref, b2_ref, wg_ref, wp_ref, o_ref, xb_ref, hid_ref, x2b_ref,
                                layer=layer)

            pltpu.emit_pipeline(step, grid=(steps,), in_specs=[act, act, p_in], out_specs=[act])(
                x_hbm, y_hbm, p_hbm.at[layer], o_hbm)

        pl.run_scoped(channel_phase, *channel_scratch)

    hbm = pl.BlockSpec(memory_space=pl.ANY)
    vmem = pl.BlockSpec(memory_space=pltpu.VMEM)
    outs = pl.pallas_call(
        body,
        in_specs=[hbm, hbm, hbm] + [vmem] * (n_small + 4) + [hbm] * (n_cw + n_hw),
        out_specs=[hbm] * (2 + n_hw),
        out_shape=[jax.ShapeDtypeStruct(x.shape, x.dtype), jax.ShapeDtypeStruct(x.shape, _BF16)]
        + [jax.ShapeDtypeStruct(w.shape[1:], _BF16) for w, _ in handoff_w],
        scratch_shapes=[pltpu.VMEM(w.shape[1:], _BF16) for w, _ in channel_w],
        compiler_params=pltpu.CompilerParams(vmem_limit_bytes=VMEM_LIMIT_BYTES),
        name=f"layer_{layer}",
    )(x, p, mixer_w_hbm, *mixer_small, *ln, *[w for w, _ in casts])
    return outs[0], outs[2:]


def kernel(x, p, ev_w_in, ev_pool_w, ev_pool_scale, ev_sgu_ln_g, ev_sgu_ln_b, ev_sgu_w, ev_sgu_b,
           ev_w_out, od_w_in, od_conv_w, od_w_out, ln1_g, ln1_b, ffn_w1, ffn_w2, ln2_g, ln2_b,
           ple_gate_w, ple_w):
    assert x.shape[1] % SEQ_TILE == 0 and SEQ_TILE % SUB_ROWS == 0 and SUB_ROWS % CHUNK == 0
    assert ln1_g.shape[0] == DEPTH and ev_w_in.shape[0] == 1 and od_w_in.shape[0] == 1
    bsz, _, d = x.shape
    ln = (ln1_g, ln1_b, ln2_g, ln2_b)
    channel_w = lambda w_out, i: [(w_out, 0), (ffn_w1, i), (ffn_w2, i), (ple_gate_w, i),
                                  (ple_w, i)]

    def even_prep(w_hbm, small, scratch):
        pool_w_ref, _, _, _, sgu_w_ref, sgu_pos_b_ref = small
        (w_in_ref, pool_bd_ref, w_tri_ref, sgu_bias_ref, ext_ref, lvl2_ref, lvl4_ref, _,
         stage_ref, sem) = scratch
        half = stage_ref.shape[0]
        loads = [pltpu.make_async_copy(w_hbm.at[0, pl.ds(i * half, half), :], stage_ref, sem)
                 for i in range(2)]

        def land(i):
            loads[i].wait()
            w_in_ref[i * half:(i + 1) * half, :] = stage_ref[...].astype(_BF16)

        loads[0].start()
        _even_mixer_prep(pool_w_ref.at[0], sgu_w_ref.at[0], sgu_pos_b_ref.at[0],
                         pool_bd_ref, w_tri_ref, sgu_bias_ref, ext_ref, lvl2_ref, lvl4_ref)
        land(0)
        loads[1].start()
        return lambda: land(1)

    def even_step(s, x_ref, small, y_ref, scratch):
        _, pool_scale_ref, sgu_g_ref, sgu_b_ref, _, _ = small
        _even_mixer_kernel(s, x_ref, pool_scale_ref, sgu_g_ref, sgu_b_ref, y_ref, *scratch[:8])

    pool_buf = pltpu.VMEM((bsz, POOL_BASE + SEQ_TILE, POOL_WIDTH), _F32)
    even_scratch = [pltpu.VMEM(ev_w_in.shape[1:], _BF16),
                    pltpu.VMEM((2, 2 * POOL_GROUP_DIM, 2 * POOL_GROUP_DIM), _BF16),
                    pltpu.VMEM((SGU_HEADS, CHUNK, CHUNK), _BF16),
                    pltpu.VMEM((CHUNK, SGU_WIDTH), _F32),
                    pool_buf, pool_buf, pool_buf, pool_buf,
                    pltpu.VMEM((ev_w_in.shape[1] // 2, ev_w_in.shape[2]), _F32),
                    pltpu.SemaphoreType.DMA(())]
    x, (w_in,) = _layer(0, x, p, ln, ev_w_in,
                        (ev_pool_w, ev_pool_scale, ev_sgu_ln_g, ev_sgu_ln_b, ev_sgu_w, ev_sgu_b),
                        even_scratch, even_prep, even_step,
                        channel_w(ev_w_out, 0), [(od_w_in, 0)])

    def odd_prep(w_hbm, small, scratch):
        w_in_ref, _, sem = scratch
        load = pltpu.make_async_copy(w_hbm, w_in_ref, sem)
        load.start()
        return load.wait

    def odd_step(s, x_ref, small, y_ref, scratch):
        w_in_ref, z_ref, _ = scratch
        _odd_mixer_kernel(s, x_ref, w_in_ref, small[0].at[0], y_ref, z_ref)

    odd_scratch = [pltpu.VMEM(w_in.shape, _BF16),
                   pltpu.VMEM((bsz, CONV_HALO + SEQ_TILE, od_conv_w.shape[-1]), _F32),
                   pltpu.SemaphoreType.DMA(())]
    x, _ = _layer(1, x, p, ln, w_in, (od_conv_w,), odd_scratch, odd_prep, odd_step,
                  channel_w(od_w_out, 1), [])
    return x
```

```python
import jax
import jax.numpy as jnp
from jax import lax
from jax.experimental import pallas as pl
from jax.experimental.pallas import tpu as pltpu

POOL_WINDOWS = (2, 4, 8, 16)
POOL_GROUP_DIM = 128
POOL_WIDTH = POOL_GROUP_DIM * len(POOL_WINDOWS)
SGU_HEADS = 4
SGU_HEAD_DIM = 128
SGU_WIDTH = SGU_HEADS * SGU_HEAD_DIM
CHUNK = 128
CONV_WIDTH = 3
LN_EPS = 1e-5
DEPTH = 2
DEEPNORM_ALPHA = (2.0 * DEPTH) ** 0.25

SUBLANES = 8
POOL_HALO = max(POOL_WINDOWS)
POOL_PAD = SUBLANES
POOL_BASE = POOL_PAD + POOL_HALO
CONV_HALO = SUBLANES

SEQ_TILE = 512
SUB_ROWS = 256
LHS_SLOTS = 2
CHANNEL_WINDOWS = 3
VMEM_LIMIT_BYTES = 60 * 1024 * 1024

_BF16 = jnp.bfloat16
_F32 = jnp.float32


def _dot(a, b):
    return jnp.dot(a, b, preferred_element_type=_F32)


def _layer_norm(x, g, b):
    mu = jnp.mean(x, axis=-1, keepdims=True)
    xc = x - mu
    var = jnp.mean(xc * xc, axis=-1, keepdims=True)
    return xc * lax.rsqrt(var + LN_EPS) * g + b


def _chains(x_ref):
    bsz, ts, _ = x_ref.shape
    return [(b, r) for b in range(bsz) for r in range(0, ts, SUB_ROWS)]


def _layer_row(ref, layer):
    return ref[layer:layer + 1, :]


def _even_mixer_prep(pool_w_ref, sgu_w_ref, sgu_pos_b_ref,
                     pool_bd_ref, w_tri_ref, sgu_bias_ref, ext_ref, lvl2_ref, lvl4_ref):
    g, d, base = POOL_GROUP_DIM, SGU_HEAD_DIM, POOL_BASE
    pool_bd_ref[...] = jnp.zeros(pool_bd_ref.shape, _BF16)
    for k in range(len(POOL_WINDOWS)):
        off = (k % 2) * g
        pool_bd_ref[k // 2, off:off + g, off:off + g] = pool_w_ref[k].astype(_BF16)
    tri = (lax.broadcasted_iota(jnp.int32, (CHUNK, CHUNK), 0)
           >= lax.broadcasted_iota(jnp.int32, (CHUNK, CHUNK), 1))
    for h in range(SGU_HEADS):
        w_tri_ref[h] = jnp.where(tri, sgu_w_ref[h], 0.0).astype(_BF16)
        pos_b = jnp.broadcast_to(sgu_pos_b_ref[h:h + 1, :], (SGU_HEAD_DIM, CHUNK))
        sgu_bias_ref[:, h * d:(h + 1) * d] = pos_b.T
    for b in range(ext_ref.shape[0]):
        ext_ref[b, 0:base, :] = jnp.zeros((base, POOL_WIDTH), _F32)
        lvl2_ref[b, 0:POOL_PAD, :] = jnp.zeros((POOL_PAD, POOL_WIDTH), _F32)
        lvl4_ref[b, 0:POOL_PAD, :] = jnp.zeros((POOL_PAD, POOL_WIDTH), _F32)


def _even_mixer_kernel(s, x_ref, pool_scale_ref, sgu_g_ref, sgu_b_ref, y_ref,
                       w_in_ref, pool_bd_ref, w_tri_ref, sgu_bias_ref,
                       ext_ref, lvl2_ref, lvl4_ref, lvl8_ref):
    bsz, ts, _ = x_ref.shape
    sub, g, d, base = SUB_ROWS, POOL_GROUP_DIM, SGU_HEAD_DIM, POOL_BASE
    chains = _chains(x_ref)

    a, u, v, pooled, vn = {}, {}, {}, {}, {}
    for ch in chains:
        b, r = ch
        xb = x_ref[b, r:r + sub, :].astype(_BF16)
        a[ch] = _dot(xb, w_in_ref[:, 0:POOL_WIDTH])
        ext_ref[b, base + r:base + r + sub, :] = a[ch]
        u[ch] = _dot(xb, w_in_ref[:, POOL_WIDTH:POOL_WIDTH + SGU_WIDTH])
        v[ch] = _dot(xb, w_in_ref[:, POOL_WIDTH + SGU_WIDTH:])

    for ch in chains:
        b, r = ch
        lo = POOL_PAD if r == 0 else base + r
        hi = base + r + sub
        own = slice(base + r, hi)
        lvl2_ref[b, lo:hi, :] = ext_ref[b, lo:hi, :] + ext_ref[b, lo - 1:hi - 1, :]
        lvl4_ref[b, lo:hi, g:] = lvl2_ref[b, lo:hi, g:] + lvl2_ref[b, lo - 2:hi - 2, g:]
        lvl8_ref[b, lo:hi, 2 * g:] = lvl4_ref[b, lo:hi, 2 * g:] + lvl4_ref[b, lo - 4:hi - 4, 2 * g:]
        sums = (
            lvl2_ref[b, own, 0:g],
            lvl4_ref[b, own, g:2 * g],
            lvl8_ref[b, own, 2 * g:3 * g],
            lvl8_ref[b, own, 3 * g:4 * g] + lvl8_ref[b, base + r - 8:hi - 8, 3 * g:4 * g],
        )
        t1 = lax.broadcasted_iota(jnp.int32, (sub, g), 0) + (s * ts + r + 1)
        parts = []
        for gi, w in enumerate(POOL_WINDOWS):
            cnt = jnp.minimum(t1, w).astype(_F32)
            parts.append((sums[gi] / cnt - a[ch][:, gi * g:(gi + 1) * g]).astype(_BF16))
        pooled[ch] = parts
        vn[ch] = _layer_norm(v[ch], sgu_g_ref[...], sgu_b_ref[...]).astype(_BF16)

    for b in range(bsz):
        ext_ref[b, POOL_PAD:base, :] = a[(b, ts - sub)][sub - POOL_HALO:sub, :]

    scale = pool_scale_ref[...]
    n_chunks = sub // CHUNK
    for ch in chains:
        b, r = ch
        for pair in range(2):
            pb = jnp.concatenate(pooled[ch][2 * pair:2 * pair + 2], axis=1)
            ya = _dot(pb, pool_bd_ref[pair]) * scale[:, 2 * g * pair:2 * g * (pair + 1)]
            y_ref[b, r:r + sub, 2 * g * pair:2 * g * (pair + 1)] = ya.astype(_BF16)
        for h in range(SGU_HEADS):
            rhs = jnp.concatenate(
                [vn[ch][c * CHUNK:(c + 1) * CHUNK, h * d:(h + 1) * d] for c in range(n_chunks)],
                axis=1)
            mixed = _dot(w_tri_ref[h], rhs)
            bias_h = sgu_bias_ref[:, h * d:(h + 1) * d]
            for c in range(n_chunks):
                m = mixed[:, c * d:(c + 1) * d] + bias_h
                yb = u[ch][c * CHUNK:(c + 1) * CHUNK, h * d:(h + 1) * d] * m
                y_ref[b, r + c * CHUNK:r + (c + 1) * CHUNK,
                      POOL_WIDTH + h * d:POOL_WIDTH + (h + 1) * d] = yb.astype(_BF16)


def _odd_mixer_kernel(s, x_ref, w_in_ref, conv_w_ref, y_ref, z_ref):
    bsz, ts, _ = x_ref.shape
    sub = SUB_ROWS
    cd = conv_w_ref.shape[-1]
    chains = _chains(x_ref)

    @pl.when(s == 0)
    def _():
        for b in range(bsz):
            z_ref[b, 0:CONV_HALO, :] = jnp.zeros((CONV_HALO, cd), _F32)

    z, gate_b = {}, {}
    for ch in chains:
        b, r = ch
        xb = x_ref[b, r:r + sub, :].astype(_BF16)
        gate_c = _dot(xb, w_in_ref[:, cd:2 * cd])
        hid = _dot(xb, w_in_ref[:, 2 * cd:3 * cd])
        z[ch] = gate_c * hid
        z_ref[b, CONV_HALO + r:CONV_HALO + r + sub, :] = z[ch]
        gate_b[ch] = _dot(xb, w_in_ref[:, 0:cd])
    cw = conv_w_ref[...]
    for ch in chains:
        b, r = ch
        conv = cw[CONV_WIDTH - 1:CONV_WIDTH, :] * z[ch]
        for k in range(CONV_WIDTH - 1):
            off = CONV_HALO + r - (CONV_WIDTH - 1) + k
            conv = conv + cw[k:k + 1, :] * z_ref[b, off:off + sub, :]
        y_ref[b, r:r + sub, :] = (gate_b[ch] * conv).astype(_BF16)
    for b in range(bsz):
        z_ref[b, 0:CONV_HALO, :] = z[(b, ts - sub)][sub - CONV_HALO:sub, :]


def _channel_kernel(x_ref, y_ref, p_ref, w_out_ref, ln1_g_ref, ln1_b_ref, w1_ref, w2_ref,
                    ln2_g_ref, ln2_b_ref, wg_ref, wp_ref, o_ref, xb_ref, hid_ref, x2b_ref,
                    *, layer):
    sub = SUB_ROWS
    gain1, bias1 = _layer_row(ln1_g_ref, layer), _layer_row(ln1_b_ref, layer)
    gain2, bias2 = _layer_row(ln2_g_ref, layer), _layer_row(ln2_b_ref, layer)
    d_ff = w1_ref.shape[1]
    ff_chunk = w1_ref.shape[0]
    chains = _chains(x_ref)

    def out_proj(ch):
        b, r = ch
        return _dot(y_ref[b, r:r + sub, :], w_out_ref[...])

    def mlp(ch, slot, mix):
        b, r = ch
        x1 = _layer_norm(DEEPNORM_ALPHA * x_ref[b, r:r + sub, :] + mix, gain1, bias1)
        xb_ref[slot] = x1.astype(_BF16)
        for c in range(d_ff // ff_chunk):
            h = _dot(xb_ref[slot], w1_ref[:, c * ff_chunk:(c + 1) * ff_chunk])
            h = jnp.maximum(h, 0.0)
            hid_ref[slot, :, c * ff_chunk:(c + 1) * ff_chunk] = (h * h).astype(_BF16)
        acc = _dot(hid_ref[slot], w2_ref[...])
        return _layer_norm(DEEPNORM_ALPHA * x1 + acc, gain2, bias2)

    def embed_gate(ch, slot, x2):
        b, r = ch
        x2b_ref[slot] = x2.astype(_BF16)
        gate = jax.nn.sigmoid(_dot(x2b_ref[slot], wg_ref[...]))
        pp = _dot(p_ref[b, r:r + sub, :].astype(_BF16), wp_ref[...])
        o_ref[b, r:r + sub, :] = x2 + gate * pp

    mix = out_proj(chains[0])
    pending = None
    for i, ch in enumerate(chains):
        next_mix = out_proj(chains[i + 1]) if i + 1 < len(chains) else None
        x2 = mlp(ch, i % LHS_SLOTS, mix)
        if pending is not None:
            embed_gate(*pending)
        pending = (ch, i % LHS_SLOTS, x2)
        mix = next_mix
    embed_gate(*pending)


def _layer(layer, x, p, ln, mixer_w_hbm, mixer_small, mixer_scratch, mixer_prep, mixer_step,
           channel_w, handoff_w):
    bsz, seq, d = x.shape
    steps = seq // SEQ_TILE
    act = pl.BlockSpec((bsz, SEQ_TILE, d), lambda s: (0, s, 0))
    deep = pl.Buffered(CHANNEL_WINDOWS)
    act_deep = pl.BlockSpec((bsz, SEQ_TILE, d), lambda s: (0, s, 0), pipeline_mode=deep)
    p_deep = pl.BlockSpec((bsz, SEQ_TILE, p.shape[-1]), lambda s: (0, s, 0), pipeline_mode=deep)
    casts = list(channel_w) + list(handoff_w)
    slabs = [w.shape[1] // steps for w, _ in casts]
    assert all(slab * steps == w.shape[1] and slab % (2 * SUBLANES) == 0
               for slab, (w, _) in zip(slabs, casts))
    cast_specs = [pl.BlockSpec((slab, w.shape[2]), lambda s: (s, 0))
                  for slab, (w, _) in zip(slabs, casts)]
    n_small, n_cw, n_hw = len(mixer_small), len(channel_w), len(handoff_w)
    d_ff = channel_w[1][0].shape[-1]
    channel_scratch = [pltpu.VMEM((LHS_SLOTS, SUB_ROWS, d), _BF16),
                       pltpu.VMEM((LHS_SLOTS, SUB_ROWS, d_ff), _BF16),
                       pltpu.VMEM((LHS_SLOTS, SUB_ROWS, d), _BF16)]

    def body(x_hbm, p_hbm, w_hbm, *refs):
        small = refs[:n_small]
        ln_refs = refs[n_small:n_small + 4]
        srcs = refs[n_small + 4:n_small + 4 + n_cw + n_hw]
        k = n_small + 4 + n_cw + n_hw
        o_hbm, y_hbm = refs[k:k + 2]
        handoff = refs[k + 2:k + 2 + n_hw]
        w_refs = refs[k + 2 + n_hw:]

        def mixer_phase(tile_ref, *scratch):
            finish_prep = mixer_prep(w_hbm, small, scratch)
            tile_ref[0] = 0

            def step(x_ref, *tiles):
                src_tiles = tiles[:n_cw + n_hw]
                y_ref = tiles[n_cw + n_hw]
                out_tiles = tiles[n_cw + n_hw + 1:]
                s = tile_ref[0]
                if finish_prep is not None:
                    pl.when(s == 0)(finish_prep)
                mixer_step(s, x_ref, small, y_ref, scratch)
                for src, dst, slab in zip(src_tiles[:n_cw], w_refs, slabs):
                    row = pl.multiple_of(s * slab, slab)
                    dst[pl.ds(row, slab), :] = src[...].astype(_BF16)
                for src, dst in zip(src_tiles[n_cw:], out_tiles):
                    dst[...] = src[...].astype(_BF16)
                tile_ref[0] = s + 1

            pltpu.emit_pipeline(step, grid=(steps,), in_specs=[act] + cast_specs,
                                out_specs=[act] + cast_specs[n_cw:])(
                x_hbm, *[src.at[idx] for src, (_, idx) in zip(srcs, casts)], y_hbm, *handoff)

        pl.run_scoped(mixer_phase, pltpu.SMEM((1,), jnp.int32), *mixer_scratch)

        def channel_phase(xb_ref, hid_ref, x2b_ref):
            w_out_ref, w1_ref, w2_ref, wg_ref, wp_ref = w_refs
            g1_ref, b1_ref, g2_ref, b2_ref = ln_refs

            def step(x_ref, y_ref, p_ref, o_ref):
                _channel_kernel(x_ref, y_ref, p_ref, w_out_ref, g1_ref, b1_ref, w1_ref, w2_ref,
                                g2_ref, b2_ref, wg_ref, wp_ref, o_ref, xb_ref, hid_ref, x2b_ref,
                                layer=layer)

            pltpu.emit_pipeline(step, grid=(steps,), in_specs=[act_deep, act_deep, p_deep],
                                out_specs=[act])(
                x_hbm, y_hbm, p_hbm.at[layer], o_hbm)

        pl.run_scoped(channel_phase, *channel_scratch)

    hbm = pl.BlockSpec(memory_space=pl.ANY)
    vmem = pl.BlockSpec(memory_space=pltpu.VMEM)
    outs = pl.pallas_call(
        body,
        in_specs=[hbm, hbm, hbm] + [vmem] * (n_small + 4) + [hbm] * (n_cw + n_hw),
        out_specs=[hbm] * (2 + n_hw),
        out_shape=[jax.ShapeDtypeStruct(x.shape, x.dtype), jax.ShapeDtypeStruct(x.shape, _BF16)]
        + [jax.ShapeDtypeStruct(w.shape[1:], _BF16) for w, _ in handoff_w],
        scratch_shapes=[pltpu.VMEM(w.shape[1:], _BF16) for w, _ in channel_w],
        compiler_params=pltpu.CompilerParams(vmem_limit_bytes=VMEM_LIMIT_BYTES),
        name=f"layer_{layer}",
    )(x, p, mixer_w_hbm, *mixer_small, *ln, *[w for w, _ in casts])
    return outs[0], outs[2:]


def kernel(x, p, ev_w_in, ev_pool_w, ev_pool_scale, ev_sgu_ln_g, ev_sgu_ln_b, ev_sgu_w, ev_sgu_b,
           ev_w_out, od_w_in, od_conv_w, od_w_out, ln1_g, ln1_b, ffn_w1, ffn_w2, ln2_g, ln2_b,
           ple_gate_w, ple_w):
    assert x.shape[1] % SEQ_TILE == 0 and SEQ_TILE % SUB_ROWS == 0 and SUB_ROWS % CHUNK == 0
    assert ln1_g.shape[0] == DEPTH and ev_w_in.shape[0] == 1 and od_w_in.shape[0] == 1
    bsz, _, d = x.shape
    ln = (ln1_g, ln1_b, ln2_g, ln2_b)
    channel_w = lambda w_out, i: [(w_out, 0), (ffn_w1, i), (ffn_w2, i), (ple_gate_w, i),
                                  (ple_w, i)]

    def even_prep(w_hbm, small, scratch):
        pool_w_ref, _, _, _, sgu_w_ref, sgu_pos_b_ref = small
        (w_in_ref, pool_bd_ref, w_tri_ref, sgu_bias_ref, ext_ref, lvl2_ref, lvl4_ref, _,
         stage_ref, sem) = scratch
        half = stage_ref.shape[0]
        loads = [pltpu.make_async_copy(w_hbm.at[0, pl.ds(i * half, half), :], stage_ref, sem)
                 for i in range(2)]

        def land(i):
            loads[i].wait()
            w_in_ref[i * half:(i + 1) * half, :] = stage_ref[...].astype(_BF16)

        loads[0].start()
        _even_mixer_prep(pool_w_ref.at[0], sgu_w_ref.at[0], sgu_pos_b_ref.at[0],
                         pool_bd_ref, w_tri_ref, sgu_bias_ref, ext_ref, lvl2_ref, lvl4_ref)
        land(0)
        loads[1].start()
        return lambda: land(1)

    def even_step(s, x_ref, small, y_ref, scratch):
        _, pool_scale_ref, sgu_g_ref, sgu_b_ref, _, _ = small
        _even_mixer_kernel(s, x_ref, pool_scale_ref, sgu_g_ref, sgu_b_ref, y_ref, *scratch[:8])

    pool_buf = pltpu.VMEM((bsz, POOL_BASE + SEQ_TILE, POOL_WIDTH), _F32)
    even_scratch = [pltpu.VMEM(ev_w_in.shape[1:], _BF16),
                    pltpu.VMEM((2, 2 * POOL_GROUP_DIM, 2 * POOL_GROUP_DIM), _BF16),
                    pltpu.VMEM((SGU_HEADS, CHUNK, CHUNK), _BF16),
                    pltpu.VMEM((CHUNK, SGU_WIDTH), _F32),
                    pool_buf, pool_buf, pool_buf, pool_buf,
                    pltpu.VMEM((ev_w_in.shape[1] // 2, ev_w_in.shape[2]), _F32),
                    pltpu.SemaphoreType.DMA(())]
    x, (w_in,) = _layer(0, x, p, ln, ev_w_in,
                        (ev_pool_w, ev_pool_scale, ev_sgu_ln_g, ev_sgu_ln_b, ev_sgu_w, ev_sgu_b),
                        even_scratch, even_prep, even_step,
                        channel_w(ev_w_out, 0), [(od_w_in, 0)])

    def odd_prep(w_hbm, small, scratch):
        w_in_ref, _, sem = scratch
        load = pltpu.make_async_copy(w_hbm, w_in_ref, sem)
        load.start()
        return load.wait

    def odd_step(s, x_ref, small, y_ref, scratch):
        w_in_ref, z_ref, _ = scratch
        _odd_mixer_kernel(s, x_ref, w_in_ref, small[0].at[0], y_ref, z_ref)

    odd_scratch = [pltpu.VMEM(w_in.shape, _BF16),
                   pltpu.VMEM((bsz, CONV_HALO + SEQ_TILE, od_conv_w.shape[-1]), _F32),
                   pltpu.SemaphoreType.DMA(())]
    x, _ = _layer(1, x, p, ln, w_in, (od_conv_w,), odd_scratch, odd_prep, odd_step,
                  channel_w(od_w_out, 1), [])
    return x
```

```python
import jax
import jax.numpy as jnp
from jax import lax
from jax.experimental import pallas as pl
from jax.experimental.pallas import tpu as pltpu

POOL_WINDOWS = (2, 4, 8, 16)
POOL_GROUP_DIM = 128
POOL_WIDTH = POOL_GROUP_DIM * len(POOL_WINDOWS)
SGU_HEADS = 4
SGU_HEAD_DIM = 128
SGU_WIDTH = SGU_HEADS * SGU_HEAD_DIM
CHUNK = 128
CONV_WIDTH = 3
LN_EPS = 1e-5
DEPTH = 2
DEEPNORM_ALPHA = (2.0 * DEPTH) ** 0.25

SUBLANES = 8
POOL_HALO = max(POOL_WINDOWS)
POOL_PAD = SUBLANES
POOL_BASE = POOL_PAD + POOL_HALO
CONV_HALO = SUBLANES

SEQ_TILE = 512
SUB_ROWS = 256
LHS_SLOTS = 2
VMEM_LIMIT_BYTES = 60 * 1024 * 1024

_BF16 = jnp.bfloat16
_F32 = jnp.float32


def _dot(a, b):
    return jnp.dot(a, b, preferred_element_type=_F32)


def _layer_norm(x, g, b):
    mu = jnp.mean(x, axis=-1, keepdims=True)
    xc = x - mu
    var = jnp.mean(xc * xc, axis=-1, keepdims=True)
    return xc * lax.rsqrt(var + LN_EPS) * g + b


def _chains(x_ref):
    bsz, ts, _ = x_ref.shape
    return [(b, r) for b in range(bsz) for r in range(0, ts, SUB_ROWS)]


def _layer_row(ref, layer):
    return ref[layer:layer + 1, :]


def _even_mixer_prep(pool_w_ref, sgu_w_ref, sgu_pos_b_ref,
                     pool_bd_ref, w_tri_ref, sgu_bias_ref, ext_ref, lvl2_ref, lvl4_ref):
    g, d, base = POOL_GROUP_DIM, SGU_HEAD_DIM, POOL_BASE
    pool_bd_ref[...] = jnp.zeros(pool_bd_ref.shape, _BF16)
    for k in range(len(POOL_WINDOWS)):
        off = (k % 2) * g
        pool_bd_ref[k // 2, off:off + g, off:off + g] = pool_w_ref[k].astype(_BF16)
    tri = (lax.broadcasted_iota(jnp.int32, (CHUNK, CHUNK), 0)
           >= lax.broadcasted_iota(jnp.int32, (CHUNK, CHUNK), 1))
    for h in range(SGU_HEADS):
        w_tri_ref[h] = jnp.where(tri, sgu_w_ref[h], 0.0).astype(_BF16)
        pos_b = jnp.broadcast_to(sgu_pos_b_ref[h:h + 1, :], (SGU_HEAD_DIM, CHUNK))
        sgu_bias_ref[:, h * d:(h + 1) * d] = pos_b.T
    for b in range(ext_ref.shape[0]):
        ext_ref[b, 0:base, :] = jnp.zeros((base, POOL_WIDTH), _F32)
        lvl2_ref[b, 0:POOL_PAD, :] = jnp.zeros((POOL_PAD, POOL_WIDTH), _F32)
        lvl4_ref[b, 0:POOL_PAD, :] = jnp.zeros((POOL_PAD, POOL_WIDTH), _F32)


def _even_mixer_kernel(s, x_ref, pool_scale_ref, sgu_g_ref, sgu_b_ref, y_ref,
                       w_in_ref, pool_bd_ref, w_tri_ref, sgu_bias_ref,
                       ext_ref, lvl2_ref, lvl4_ref, lvl8_ref):
    bsz, ts, _ = x_ref.shape
    sub, g, d, base = SUB_ROWS, POOL_GROUP_DIM, SGU_HEAD_DIM, POOL_BASE
    chains = _chains(x_ref)

    a, u, v, pooled, vn = {}, {}, {}, {}, {}
    for ch in chains:
        b, r = ch
        xb = x_ref[b, r:r + sub, :].astype(_BF16)
        a[ch] = _dot(xb, w_in_ref[:, 0:POOL_WIDTH])
        ext_ref[b, base + r:base + r + sub, :] = a[ch]
        u[ch] = _dot(xb, w_in_ref[:, POOL_WIDTH:POOL_WIDTH + SGU_WIDTH])
        v[ch] = _dot(xb, w_in_ref[:, POOL_WIDTH + SGU_WIDTH:])

    for ch in chains:
        b, r = ch
        lo = POOL_PAD if r == 0 else base + r
        hi = base + r + sub
        own = slice(base + r, hi)
        lvl2_ref[b, lo:hi, :] = ext_ref[b, lo:hi, :] + ext_ref[b, lo - 1:hi - 1, :]
        lvl4_ref[b, lo:hi, g:] = lvl2_ref[b, lo:hi, g:] + lvl2_ref[b, lo - 2:hi - 2, g:]
        lvl8_ref[b, lo:hi, 2 * g:] = lvl4_ref[b, lo:hi, 2 * g:] + lvl4_ref[b, lo - 4:hi - 4, 2 * g:]
        sums = (
            lvl2_ref[b, own, 0:g],
            lvl4_ref[b, own, g:2 * g],
            lvl8_ref[b, own, 2 * g:3 * g],
            lvl8_ref[b, own, 3 * g:4 * g] + lvl8_ref[b, base + r - 8:hi - 8, 3 * g:4 * g],
        )
        t1 = lax.broadcasted_iota(jnp.int32, (sub, g), 0) + (s * ts + r + 1)
        parts = []
        for gi, w in enumerate(POOL_WINDOWS):
            cnt = jnp.minimum(t1, w).astype(_F32)
            parts.append((sums[gi] / cnt - a[ch][:, gi * g:(gi + 1) * g]).astype(_BF16))
        pooled[ch] = parts
        vn[ch] = _layer_norm(v[ch], sgu_g_ref[...], sgu_b_ref[...]).astype(_BF16)

    for b in range(bsz):
        ext_ref[b, POOL_PAD:base, :] = a[(b, ts - sub)][sub - POOL_HALO:sub, :]

    scale = pool_scale_ref[...]
    n_chunks = sub // CHUNK
    for ch in chains:
        b, r = ch
        for pair in range(2):
            pb = jnp.concatenate(pooled[ch][2 * pair:2 * pair + 2], axis=1)
            ya = _dot(pb, pool_bd_ref[pair]) * scale[:, 2 * g * pair:2 * g * (pair + 1)]
            y_ref[b, r:r + sub, 2 * g * pair:2 * g * (pair + 1)] = ya.astype(_BF16)
        for h in range(SGU_HEADS):
            rhs = jnp.concatenate(
                [vn[ch][c * CHUNK:(c + 1) * CHUNK, h * d:(h + 1) * d] for c in range(n_chunks)],
                axis=1)
            mixed = _dot(w_tri_ref[h], rhs)
            bias_h = sgu_bias_ref[:, h * d:(h + 1) * d]
            for c in range(n_chunks):
                m = mixed[:, c * d:(c + 1) * d] + bias_h
                yb = u[ch][c * CHUNK:(c + 1) * CHUNK, h * d:(h + 1) * d] * m
                y_ref[b, r + c * CHUNK:r + (c + 1) * CHUNK,
                      POOL_WIDTH + h * d:POOL_WIDTH + (h + 1) * d] = yb.astype(_BF16)


def _odd_mixer_kernel(s, x_ref, w_in_ref, conv_w_ref, y_ref, z_ref):
    bsz, ts, _ = x_ref.shape
    sub = SUB_ROWS
    cd = conv_w_ref.shape[-1]
    chains = _chains(x_ref)

    @pl.when(s == 0)
    def _():
        for b in range(bsz):
            z_ref[b, 0:CONV_HALO, :] = jnp.zeros((CONV_HALO, cd), _F32)

    z, gate_b = {}, {}
    for ch in chains:
        b, r = ch
        xb = x_ref[b, r:r + sub, :].astype(_BF16)
        gate_c = _dot(xb, w_in_ref[:, cd:2 * cd])
        hid = _dot(xb, w_in_ref[:, 2 * cd:3 * cd])
        z[ch] = gate_c * hid
        z_ref[b, CONV_HALO + r:CONV_HALO + r + sub, :] = z[ch]
        gate_b[ch] = _dot(xb, w_in_ref[:, 0:cd])
    cw = conv_w_ref[...]
    for ch in chains:
        b, r = ch
        conv = cw[CONV_WIDTH - 1:CONV_WIDTH, :] * z[ch]
        for k in range(CONV_WIDTH - 1):
            off = CONV_HALO + r - (CONV_WIDTH - 1) + k
            conv = conv + cw[k:k + 1, :] * z_ref[b, off:off + sub, :]
        y_ref[b, r:r + sub, :] = (gate_b[ch] * conv).astype(_BF16)
    for b in range(bsz):
        z_ref[b, 0:CONV_HALO, :] = z[(b, ts - sub)][sub - CONV_HALO:sub, :]


def _channel_kernel(x_ref, y_ref, p_ref, w_out_ref, ln1_g_ref, ln1_b_ref, w1_ref, w2_ref,
                    ln2_g_ref, ln2_b_ref, wg_ref, wp_ref, o_ref, xb_ref, hid_ref, x2b_ref,
                    *, layer):
    sub = SUB_ROWS
    gain1, bias1 = _layer_row(ln1_g_ref, layer), _layer_row(ln1_b_ref, layer)
    gain2, bias2 = _layer_row(ln2_g_ref, layer), _layer_row(ln2_b_ref, layer)
    d_ff = w1_ref.shape[1]
    ff_chunk = w1_ref.shape[0]
    chains = _chains(x_ref)

    def out_proj(ch):
        b, r = ch
        return _dot(y_ref[b, r:r + sub, :], w_out_ref[...])

    def mlp(ch, slot, mix):
        b, r = ch
        x1 = _layer_norm(DEEPNORM_ALPHA * x_ref[b, r:r + sub, :] + mix, gain1, bias1)
        xb_ref[slot] = x1.astype(_BF16)
        for c in range(d_ff // ff_chunk):
            h = _dot(xb_ref[slot], w1_ref[:, c * ff_chunk:(c + 1) * ff_chunk])
            h = jnp.maximum(h, 0.0)
            hid_ref[slot, :, c * ff_chunk:(c + 1) * ff_chunk] = (h * h).astype(_BF16)
        acc = _dot(hid_ref[slot], w2_ref[...])
        return _layer_norm(DEEPNORM_ALPHA * x1 + acc, gain2, bias2)

    def embed_gate(ch, slot, x2):
        b, r = ch
        x2b_ref[slot] = x2.astype(_BF16)
        gate = jax.nn.sigmoid(_dot(x2b_ref[slot], wg_ref[...]))
        pp = _dot(p_ref[b, r:r + sub, :].astype(_BF16), wp_ref[...])
        o_ref[b, r:r + sub, :] = x2 + gate * pp

    mix = out_proj(chains[0])
    pending = None
    for i, ch in enumerate(chains):
        next_mix = out_proj(chains[i + 1]) if i + 1 < len(chains) else None
        x2 = mlp(ch, i % LHS_SLOTS, mix)
        if pending is not None:
            embed_gate(*pending)
        pending = (ch, i % LHS_SLOTS, x2)
        mix = next_mix
    embed_gate(*pending)


def _layer(layer, x, p, ln, mixer_w_hbm, mixer_small, mixer_scratch, mixer_prep, mixer_step,
           channel_w, handoff_w):
    bsz, seq, d = x.shape
    steps = seq // SEQ_TILE
    act = pl.BlockSpec((bsz, SEQ_TILE, d), lambda s: (0, s, 0))
    p_in = pl.BlockSpec((bsz, SEQ_TILE, p.shape[-1]), lambda s: (0, s, 0))
    casts = list(channel_w) + list(handoff_w)
    slabs = [w.shape[1] // steps for w, _ in casts]
    assert all(slab * steps == w.shape[1] and slab % (2 * SUBLANES) == 0
               for slab, (w, _) in zip(slabs, casts))
    cast_specs = [pl.BlockSpec((slab, w.shape[2]), lambda s: (s, 0))
                  for slab, (w, _) in zip(slabs, casts)]
    n_small, n_cw, n_hw = len(mixer_small), len(channel_w), len(handoff_w)
    d_ff = channel_w[1][0].shape[-1]
    channel_scratch = [pltpu.VMEM((LHS_SLOTS, SUB_ROWS, d), _BF16),
                       pltpu.VMEM((LHS_SLOTS, SUB_ROWS, d_ff), _BF16),
                       pltpu.VMEM((LHS_SLOTS, SUB_ROWS, d), _BF16)]

    def body(x_hbm, p_hbm, w_hbm, *refs):
        small = refs[:n_small]
        ln_refs = refs[n_small:n_small + 4]
        srcs = refs[n_small + 4:n_small + 4 + n_cw + n_hw]
        k = n_small + 4 + n_cw + n_hw
        o_hbm, y_hbm = refs[k:k + 2]
        handoff = refs[k + 2:k + 2 + n_hw]
        w_refs = refs[k + 2 + n_hw:]

        def mixer_phase(tile_ref, slab_sem, *scratch):
            slab_bufs, scratch = scratch[:n_cw], scratch[n_cw:]
            finish_prep = mixer_prep(w_hbm, small, scratch)
            tile_ref[0] = 0

            def slab_loads(s, slot):
                return [pltpu.make_async_copy(
                            src.at[idx, pl.ds(pl.multiple_of(s * slab, slab), slab), :],
                            buf.at[slot], slab_sem.at[i, slot])
                        for i, (src, (_, idx), buf, slab)
                        in enumerate(zip(srcs, channel_w, slab_bufs, slabs))]

            for load in slab_loads(0, 0):
                load.start()

            def step(x_ref, *tiles):
                src_tiles, y_ref, out_tiles = tiles[:n_hw], tiles[n_hw], tiles[n_hw + 1:]
                s = tile_ref[0]
                slot = s % 2

                @pl.when(s + 1 < steps)
                def _():
                    for load in slab_loads(s + 1, 1 - slot):
                        load.start()

                if finish_prep is not None:
                    pl.when(s == 0)(finish_prep)
                mixer_step(s, x_ref, small, y_ref, scratch)
                for load, buf, dst, slab in zip(slab_loads(s, slot), slab_bufs, w_refs, slabs):
                    load.wait()
                    row = pl.multiple_of(s * slab, slab)
                    dst[pl.ds(row, slab), :] = buf[slot].astype(_BF16)
                for src, dst in zip(src_tiles, out_tiles):
                    dst[...] = src[...].astype(_BF16)
                tile_ref[0] = s + 1

            pltpu.emit_pipeline(step, grid=(steps,), in_specs=[act] + cast_specs[n_cw:],
                                out_specs=[act] + cast_specs[n_cw:])(
                x_hbm, *[src.at[idx] for src, (_, idx) in zip(srcs[n_cw:], handoff_w)],
                y_hbm, *handoff)

        pl.run_scoped(mixer_phase, pltpu.SMEM((1,), jnp.int32),
                      pltpu.SemaphoreType.DMA((n_cw, 2)),
                      *[pltpu.VMEM((2, slab, w.shape[2]), _F32)
                        for slab, (w, _) in zip(slabs, channel_w)],
                      *mixer_scratch)

        def channel_phase(xb_ref, hid_ref, x2b_ref):
            w_out_ref, w1_ref, w2_ref, wg_ref, wp_ref = w_refs
            g1_ref, b1_ref, g2_ref, b2_ref = ln_refs

            def step(x_ref, y_ref, p_ref, o_ref):
                _channel_kernel(x_ref, y_ref, p_ref, w_out_ref, g1_ref, b1_ref, w1_ref, w2_ref,
                                g2_ref, b2_ref, wg_ref, wp_ref, o_ref, xb_ref, hid_ref, x2b_ref,
                                layer=layer)

            pltpu.emit_pipeline(step, grid=(steps,), in_specs=[act, act, p_in], out_specs=[act])(
                x_hbm, y_hbm, p_hbm.at[layer], o_hbm)

        pl.run_scoped(channel_phase, *channel_scratch)

    hbm = pl.BlockSpec(memory_space=pl.ANY)
    vmem = pl.BlockSpec(memory_space=pltpu.VMEM)
    outs = pl.pallas_call(
        body,
        in_specs=[hbm, hbm, hbm] + [vmem] * (n_small + 4) + [hbm] * (n_cw + n_hw),
        out_specs=[hbm] * (2 + n_hw),
        out_shape=[jax.ShapeDtypeStruct(x.shape, x.dtype), jax.ShapeDtypeStruct(x.shape, _BF16)]
        + [jax.ShapeDtypeStruct(w.shape[1:], _BF16) for w, _ in handoff_w],
        scratch_shapes=[pltpu.VMEM(w.shape[1:], _BF16) for w, _ in channel_w],
        compiler_params=pltpu.CompilerParams(vmem_limit_bytes=VMEM_LIMIT_BYTES),
        name=f"layer_{layer}",
    )(x, p, mixer_w_hbm, *mixer_small, *ln, *[w for w, _ in casts])
    return outs[0], outs[2:]


def kernel(x, p, ev_w_in, ev_pool_w, ev_pool_scale, ev_sgu_ln_g, ev_sgu_ln_b, ev_sgu_w, ev_sgu_b,
           ev_w_out, od_w_in, od_conv_w, od_w_out, ln1_g, ln1_b, ffn_w1, ffn_w2, ln2_g, ln2_b,
           ple_gate_w, ple_w):
    assert x.shape[1] % SEQ_TILE == 0 and SEQ_TILE % SUB_ROWS == 0 and SUB_ROWS % CHUNK == 0
    assert ln1_g.shape[0] == DEPTH and ev_w_in.shape[0] == 1 and od_w_in.shape[0] == 1
    bsz, _, d = x.shape
    ln = (ln1_g, ln1_b, ln2_g, ln2_b)
    channel_w = lambda w_out, i: [(w_out, 0), (ffn_w1, i), (ffn_w2, i), (ple_gate_w, i),
                                  (ple_w, i)]

    def even_prep(w_hbm, small, scratch):
        pool_w_ref, _, _, _, sgu_w_ref, sgu_pos_b_ref = small
        (w_in_ref, pool_bd_ref, w_tri_ref, sgu_bias_ref, ext_ref, lvl2_ref, lvl4_ref, _,
         stage_ref, sem) = scratch
        half = stage_ref.shape[0]
        loads = [pltpu.make_async_copy(w_hbm.at[0, pl.ds(i * half, half), :], stage_ref, sem)
                 for i in range(2)]

        def land(i):
            loads[i].wait()
            w_in_ref[i * half:(i + 1) * half, :] = stage_ref[...].astype(_BF16)

        loads[0].start()
        _even_mixer_prep(pool_w_ref.at[0], sgu_w_ref.at[0], sgu_pos_b_ref.at[0],
                         pool_bd_ref, w_tri_ref, sgu_bias_ref, ext_ref, lvl2_ref, lvl4_ref)
        land(0)
        loads[1].start()
        return lambda: land(1)

    def even_step(s, x_ref, small, y_ref, scratch):
        _, pool_scale_ref, sgu_g_ref, sgu_b_ref, _, _ = small
        _even_mixer_kernel(s, x_ref, pool_scale_ref, sgu_g_ref, sgu_b_ref, y_ref, *scratch[:8])

    pool_buf = pltpu.VMEM((bsz, POOL_BASE + SEQ_TILE, POOL_WIDTH), _F32)
    even_scratch = [pltpu.VMEM(ev_w_in.shape[1:], _BF16),
                    pltpu.VMEM((2, 2 * POOL_GROUP_DIM, 2 * POOL_GROUP_DIM), _BF16),
                    pltpu.VMEM((SGU_HEADS, CHUNK, CHUNK), _BF16),
                    pltpu.VMEM((CHUNK, SGU_WIDTH), _F32),
                    pool_buf, pool_buf, pool_buf, pool_buf,
                    pltpu.VMEM((ev_w_in.shape[1] // 2, ev_w_in.shape[2]), _F32),
                    pltpu.SemaphoreType.DMA(())]
    x, (w_in,) = _layer(0, x, p, ln, ev_w_in,
                        (ev_pool_w, ev_pool_scale, ev_sgu_ln_g, ev_sgu_ln_b, ev_sgu_w, ev_sgu_b),
                        even_scratch, even_prep, even_step,
                        channel_w(ev_w_out, 0), [(od_w_in, 0)])

    def odd_prep(w_hbm, small, scratch):
        w_in_ref, _, sem = scratch
        load = pltpu.make_async_copy(w_hbm, w_in_ref, sem)
        load.start()
        return load.wait

    def odd_step(s, x_ref, small, y_ref, scratch):
        w_in_ref, z_ref, _ = scratch
        _odd_mixer_kernel(s, x_ref, w_in_ref, small[0].at[0], y_ref, z_ref)

    odd_scratch = [pltpu.VMEM(w_in.shape, _BF16),
                   pltpu.VMEM((bsz, CONV_HALO + SEQ_TILE, od_conv_w.shape[-1]), _F32),
                   pltpu.SemaphoreType.DMA(())]
    x, _ = _layer(1, x, p, ln, w_in, (od_conv_w,), odd_scratch, odd_prep, odd_step,
                  channel_w(od_w_out, 1), [])
    return x
```

```python
import jax
import jax.numpy as jnp
from jax import lax
from jax.experimental import pallas as pl
from jax.experimental.pallas import tpu as pltpu

POOL_WINDOWS = (2, 4, 8, 16)
POOL_GROUP_DIM = 128
POOL_WIDTH = POOL_GROUP_DIM * len(POOL_WINDOWS)
SGU_HEADS = 4
SGU_HEAD_DIM = 128
SGU_WIDTH = SGU_HEADS * SGU_HEAD_DIM
CHUNK = 128
CONV_WIDTH = 3
LN_EPS = 1e-5
DEPTH = 2
DEEPNORM_ALPHA = (2.0 * DEPTH) ** 0.25

SUBLANES = 8
POOL_HALO = max(POOL_WINDOWS)
POOL_PAD = SUBLANES
POOL_BASE = POOL_PAD + POOL_HALO
CONV_HALO = SUBLANES

SEQ_TILE = 512
SUB_ROWS = 256
LHS_SLOTS = 2
VMEM_LIMIT_BYTES = 60 * 1024 * 1024

_BF16 = jnp.bfloat16
_F32 = jnp.float32


def _dot(a, b):
    return jnp.dot(a, b, preferred_element_type=_F32)


def _layer_norm(x, g, b):
    mu = jnp.mean(x, axis=-1, keepdims=True)
    xc = x - mu
    var = jnp.mean(xc * xc, axis=-1, keepdims=True)
    return xc * lax.rsqrt(var + LN_EPS) * g + b


def _chains(x_ref):
    bsz, ts, _ = x_ref.shape
    return [(b, r) for b in range(bsz) for r in range(0, ts, SUB_ROWS)]


def _layer_row(ref, layer):
    return ref[layer:layer + 1, :]


def _even_mixer_prep(pool_w_ref, sgu_w_ref, sgu_pos_b_ref,
                     pool_bd_ref, w_tri_ref, sgu_bias_ref, ext_ref, lvl2_ref, lvl4_ref):
    g, d, base = POOL_GROUP_DIM, SGU_HEAD_DIM, POOL_BASE
    pool_bd_ref[...] = jnp.zeros(pool_bd_ref.shape, _BF16)
    for k in range(len(POOL_WINDOWS)):
        off = (k % 2) * g
        pool_bd_ref[k // 2, off:off + g, off:off + g] = pool_w_ref[k].astype(_BF16)
    tri = (lax.broadcasted_iota(jnp.int32, (CHUNK, CHUNK), 0)
           >= lax.broadcasted_iota(jnp.int32, (CHUNK, CHUNK), 1))
    for h in range(SGU_HEADS):
        w_tri_ref[h] = jnp.where(tri, sgu_w_ref[h], 0.0).astype(_BF16)
        pos_b = jnp.broadcast_to(sgu_pos_b_ref[h:h + 1, :], (SGU_HEAD_DIM, CHUNK))
        sgu_bias_ref[:, h * d:(h + 1) * d] = pos_b.T
    for b in range(ext_ref.shape[0]):
        ext_ref[b, 0:base, :] = jnp.zeros((base, POOL_WIDTH), _F32)
        lvl2_ref[b, 0:POOL_PAD, :] = jnp.zeros((POOL_PAD, POOL_WIDTH), _F32)
        lvl4_ref[b, 0:POOL_PAD, :] = jnp.zeros((POOL_PAD, POOL_WIDTH), _F32)


def _even_mixer_kernel(s, x_ref, pool_scale_ref, sgu_g_ref, sgu_b_ref, y_ref,
                       w_in_ref, pool_bd_ref, w_tri_ref, sgu_bias_ref,
                       ext_ref, lvl2_ref, lvl4_ref, lvl8_ref):
    bsz, ts, _ = x_ref.shape
    sub, g, d, base = SUB_ROWS, POOL_GROUP_DIM, SGU_HEAD_DIM, POOL_BASE
    chains = _chains(x_ref)

    a, u, v, pooled, vn = {}, {}, {}, {}, {}
    for ch in chains:
        b, r = ch
        xb = x_ref[b, r:r + sub, :].astype(_BF16)
        a[ch] = _dot(xb, w_in_ref[:, 0:POOL_WIDTH])
        ext_ref[b, base + r:base + r + sub, :] = a[ch]
        u[ch] = _dot(xb, w_in_ref[:, POOL_WIDTH:POOL_WIDTH + SGU_WIDTH])
        v[ch] = _dot(xb, w_in_ref[:, POOL_WIDTH + SGU_WIDTH:])

    for ch in chains:
        b, r = ch
        lo = POOL_PAD if r == 0 else base + r
        hi = base + r + sub
        own = slice(base + r, hi)
        lvl2_ref[b, lo:hi, :] = ext_ref[b, lo:hi, :] + ext_ref[b, lo - 1:hi - 1, :]
        lvl4_ref[b, lo:hi, g:] = lvl2_ref[b, lo:hi, g:] + lvl2_ref[b, lo - 2:hi - 2, g:]
        lvl8_ref[b, lo:hi, 2 * g:] = lvl4_ref[b, lo:hi, 2 * g:] + lvl4_ref[b, lo - 4:hi - 4, 2 * g:]
        sums = (
            lvl2_ref[b, own, 0:g],
            lvl4_ref[b, own, g:2 * g],
            lvl8_ref[b, own, 2 * g:3 * g],
            lvl8_ref[b, own, 3 * g:4 * g] + lvl8_ref[b, base + r - 8:hi - 8, 3 * g:4 * g],
        )
        t1 = lax.broadcasted_iota(jnp.int32, (sub, g), 0) + (s * ts + r + 1)
        parts = []
        for gi, w in enumerate(POOL_WINDOWS):
            cnt = jnp.minimum(t1, w).astype(_F32)
            parts.append((sums[gi] / cnt - a[ch][:, gi * g:(gi + 1) * g]).astype(_BF16))
        pooled[ch] = parts
        vn[ch] = _layer_norm(v[ch], sgu_g_ref[...], sgu_b_ref[...]).astype(_BF16)

    for b in range(bsz):
        ext_ref[b, POOL_PAD:base, :] = a[(b, ts - sub)][sub - POOL_HALO:sub, :]

    scale = pool_scale_ref[...]
    n_chunks = sub // CHUNK
    for ch in chains:
        b, r = ch
        for pair in range(2):
            pb = jnp.concatenate(pooled[ch][2 * pair:2 * pair + 2], axis=1)
            ya = _dot(pb, pool_bd_ref[pair]) * scale[:, 2 * g * pair:2 * g * (pair + 1)]
            y_ref[b, r:r + sub, 2 * g * pair:2 * g * (pair + 1)] = ya.astype(_BF16)
        for h in range(SGU_HEADS):
            rhs = jnp.concatenate(
                [vn[ch][c * CHUNK:(c + 1) * CHUNK, h * d:(h + 1) * d] for c in range(n_chunks)],
                axis=1)
            mixed = _dot(w_tri_ref[h], rhs)
            bias_h = sgu_bias_ref[:, h * d:(h + 1) * d]
            for c in range(n_chunks):
                m = mixed[:, c * d:(c + 1) * d] + bias_h
                yb = u[ch][c * CHUNK:(c + 1) * CHUNK, h * d:(h + 1) * d] * m
                y_ref[b, r + c * CHUNK:r + (c + 1) * CHUNK,
                      POOL_WIDTH + h * d:POOL_WIDTH + (h + 1) * d] = yb.astype(_BF16)


def _odd_mixer_kernel(s, x_ref, w_in_ref, conv_w_ref, y_ref, z_ref):
    bsz, ts, _ = x_ref.shape
    sub = SUB_ROWS
    cd = conv_w_ref.shape[-1]
    chains = _chains(x_ref)

    @pl.when(s == 0)
    def _():
        for b in range(bsz):
            z_ref[b, 0:CONV_HALO, :] = jnp.zeros((CONV_HALO, cd), _F32)

    z, gate_b = {}, {}
    for ch in chains:
        b, r = ch
        xb = x_ref[b, r:r + sub, :].astype(_BF16)
        gate_c = _dot(xb, w_in_ref[:, cd:2 * cd])
        hid = _dot(xb, w_in_ref[:, 2 * cd:3 * cd])
        z[ch] = gate_c * hid
        z_ref[b, CONV_HALO + r:CONV_HALO + r + sub, :] = z[ch]
        gate_b[ch] = _dot(xb, w_in_ref[:, 0:cd])
    cw = conv_w_ref[...]
    for ch in chains:
        b, r = ch
        conv = cw[CONV_WIDTH - 1:CONV_WIDTH, :] * z[ch]
        for k in range(CONV_WIDTH - 1):
            off = CONV_HALO + r - (CONV_WIDTH - 1) + k
            conv = conv + cw[k:k + 1, :] * z_ref[b, off:off + sub, :]
        y_ref[b, r:r + sub, :] = (gate_b[ch] * conv).astype(_BF16)
    for b in range(bsz):
        z_ref[b, 0:CONV_HALO, :] = z[(b, ts - sub)][sub - CONV_HALO:sub, :]


def _channel_kernel(x_ref, y_ref, p_ref, w_out_ref, ln1_g_ref, ln1_b_ref, w1_ref, w2_ref,
                    ln2_g_ref, ln2_b_ref, wg_ref, wp_ref, o_ref, xb_ref, hid_ref, x2b_ref,
                    *, layer):
    sub = SUB_ROWS
    gain1, bias1 = _layer_row(ln1_g_ref, layer), _layer_row(ln1_b_ref, layer)
    gain2, bias2 = _layer_row(ln2_g_ref, layer), _layer_row(ln2_b_ref, layer)
    d_ff = w1_ref.shape[1]
    ff_chunk = w1_ref.shape[0]
    chains = _chains(x_ref)

    def out_proj(ch):
        b, r = ch
        return _dot(y_ref[b, r:r + sub, :], w_out_ref[...])

    def mlp(ch, slot, mix):
        b, r = ch
        x1 = _layer_norm(DEEPNORM_ALPHA * x_ref[b, r:r + sub, :] + mix, gain1, bias1)
        xb_ref[slot] = x1.astype(_BF16)
        for c in range(d_ff // ff_chunk):
            h = _dot(xb_ref[slot], w1_ref[:, c * ff_chunk:(c + 1) * ff_chunk])
            h = jnp.maximum(h, 0.0)
            hid_ref[slot, :, c * ff_chunk:(c + 1) * ff_chunk] = (h * h).astype(_BF16)
        acc = _dot(hid_ref[slot], w2_ref[...])
        return _layer_norm(DEEPNORM_ALPHA * x1 + acc, gain2, bias2)

    def embed_gate(ch, slot, x2):
        b, r = ch
        x2b_ref[slot] = x2.astype(_BF16)
        gate = jax.nn.sigmoid(_dot(x2b_ref[slot], wg_ref[...]))
        pp = _dot(p_ref[b, r:r + sub, :].astype(_BF16), wp_ref[...])
        o_ref[b, r:r + sub, :] = x2 + gate * pp

    mix = out_proj(chains[0])
    pending = None
    for i, ch in enumerate(chains):
        next_mix = out_proj(chains[i + 1]) if i + 1 < len(chains) else None
        x2 = mlp(ch, i % LHS_SLOTS, mix)
        if pending is not None:
            embed_gate(*pending)
        pending = (ch, i % LHS_SLOTS, x2)
        mix = next_mix
    embed_gate(*pending)


def _layer(layer, x, p, ln, mixer_w_hbm, mixer_small, mixer_scratch, mixer_prep, mixer_step,
           channel_w, handoff_w):
    bsz, seq, d = x.shape
    steps = seq // SEQ_TILE
    act = pl.BlockSpec((bsz, SEQ_TILE, d), lambda s: (0, s, 0))
    p_in = pl.BlockSpec((bsz, SEQ_TILE, p.shape[-1]), lambda s: (0, s, 0))
    casts = list(channel_w) + list(handoff_w)
    slabs = [w.shape[1] // steps for w, _ in casts]
    assert all(slab * steps == w.shape[1] and slab % (2 * SUBLANES) == 0
               for slab, (w, _) in zip(slabs, casts))
    cast_specs = [pl.BlockSpec((slab, w.shape[2]), lambda s: (s, 0))
                  for slab, (w, _) in zip(slabs, casts)]
    n_small, n_cw, n_hw = len(mixer_small), len(channel_w), len(handoff_w)
    d_ff = channel_w[1][0].shape[-1]
    channel_scratch = [pltpu.VMEM((LHS_SLOTS, SUB_ROWS, d), _BF16),
                       pltpu.VMEM((LHS_SLOTS, SUB_ROWS, d_ff), _BF16),
                       pltpu.VMEM((LHS_SLOTS, SUB_ROWS, d), _BF16)]

    def body(x_hbm, p_hbm, w_hbm, *refs):
        small = refs[:n_small]
        ln_refs = refs[n_small:n_small + 4]
        srcs = refs[n_small + 4:n_small + 4 + n_cw + n_hw]
        k = n_small + 4 + n_cw + n_hw
        o_hbm, y_hbm = refs[k:k + 2]
        handoff = refs[k + 2:k + 2 + n_hw]
        w_refs = refs[k + 2 + n_hw:]

        def mixer_phase(tile_ref, slab_sem, *scratch):
            slab_bufs, scratch = scratch[:n_cw], scratch[n_cw:]
            finish_prep = mixer_prep(w_hbm, small, scratch)
            tile_ref[0] = 0

            def slab_loads(s):
                return [pltpu.make_async_copy(
                            src.at[idx, pl.ds(pl.multiple_of(s * slab, slab), slab), :],
                            buf, slab_sem.at[i])
                        for i, (src, (_, idx), buf, slab)
                        in enumerate(zip(srcs, channel_w, slab_bufs, slabs))]

            def step(x_ref, *tiles):
                src_tiles, y_ref, out_tiles = tiles[:n_hw], tiles[n_hw], tiles[n_hw + 1:]
                s = tile_ref[0]
                for load in slab_loads(s):
                    load.start()
                if finish_prep is not None:
                    pl.when(s == 0)(finish_prep)
                mixer_step(s, x_ref, small, y_ref, scratch)
                for load, buf, dst, slab in zip(slab_loads(s), slab_bufs, w_refs, slabs):
                    load.wait()
                    row = pl.multiple_of(s * slab, slab)
                    dst[pl.ds(row, slab), :] = buf[...].astype(_BF16)
                for src, dst in zip(src_tiles, out_tiles):
                    dst[...] = src[...].astype(_BF16)
                tile_ref[0] = s + 1

            pltpu.emit_pipeline(step, grid=(steps,), in_specs=[act] + cast_specs[n_cw:],
                                out_specs=[act] + cast_specs[n_cw:])(
                x_hbm, *[src.at[idx] for src, (_, idx) in zip(srcs[n_cw:], handoff_w)],
                y_hbm, *handoff)

        pl.run_scoped(mixer_phase, pltpu.SMEM((1,), jnp.int32),
                      pltpu.SemaphoreType.DMA((n_cw,)),
                      *[pltpu.VMEM((slab, w.shape[2]), _F32)
                        for slab, (w, _) in zip(slabs, channel_w)],
                      *mixer_scratch)

        def channel_phase(xb_ref, hid_ref, x2b_ref):
            w_out_ref, w1_ref, w2_ref, wg_ref, wp_ref = w_refs
            g1_ref, b1_ref, g2_ref, b2_ref = ln_refs

            def step(x_ref, y_ref, p_ref, o_ref):
                _channel_kernel(x_ref, y_ref, p_ref, w_out_ref, g1_ref, b1_ref, w1_ref, w2_ref,
                                g2_ref, b2_ref, wg_ref, wp_ref, o_ref, xb_ref, hid_ref, x2b_ref,
                                layer=layer)

            pltpu.emit_pipeline(step, grid=(steps,), in_specs=[act, act, p_in], out_specs=[act])(
                x_hbm, y_hbm, p_hbm.at[layer], o_hbm)

        pl.run_scoped(channel_phase, *channel_scratch)

    hbm = pl.BlockSpec(memory_space=pl.ANY)
    vmem = pl.BlockSpec(memory_space=pltpu.VMEM)
    outs = pl.pallas_call(
        body,
        in_specs=[hbm, hbm, hbm] + [vmem] * (n_small + 4) + [hbm] * (n_cw + n_hw),
        out_specs=[hbm] * (2 + n_hw),
        out_shape=[jax.ShapeDtypeStruct(x.shape, x.dtype), jax.ShapeDtypeStruct(x.shape, _BF16)]
        + [jax.ShapeDtypeStruct(w.shape[1:], _BF16) for w, _ in handoff_w],
        scratch_shapes=[pltpu.VMEM(w.shape[1:], _BF16) for w, _ in channel_w],
        compiler_params=pltpu.CompilerParams(vmem_limit_bytes=VMEM_LIMIT_BYTES),
        name=f"layer_{layer}",
    )(x, p, mixer_w_hbm, *mixer_small, *ln, *[w for w, _ in casts])
    return outs[0], outs[2:]


def kernel(x, p, ev_w_in, ev_pool_w, ev_pool_scale, ev_sgu_ln_g, ev_sgu_ln_b, ev_sgu_w, ev_sgu_b,
           ev_w_out, od_w_in, od_conv_w, od_w_out, ln1_g, ln1_b, ffn_w1, ffn_w2, ln2_g, ln2_b,
           ple_gate_w, ple_w):
    assert x.shape[1] % SEQ_TILE == 0 and SEQ_TILE % SUB_ROWS == 0 and SUB_ROWS % CHUNK == 0
    assert ln1_g.shape[0] == DEPTH and ev_w_in.shape[0] == 1 and od_w_in.shape[0] == 1
    bsz, _, d = x.shape
    ln = (ln1_g, ln1_b, ln2_g, ln2_b)
    channel_w = lambda w_out, i: [(w_out, 0), (ffn_w1, i), (ffn_w2, i), (ple_gate_w, i),
                                  (ple_w, i)]

    def even_prep(w_hbm, small, scratch):
        pool_w_ref, _, _, _, sgu_w_ref, sgu_pos_b_ref = small
        (w_in_ref, pool_bd_ref, w_tri_ref, sgu_bias_ref, ext_ref, lvl2_ref, lvl4_ref, _,
         stage_ref, sem) = scratch
        half = stage_ref.shape[0]
        loads = [pltpu.make_async_copy(w_hbm.at[0, pl.ds(i * half, half), :], stage_ref, sem)
                 for i in range(2)]

        def land(i):
            loads[i].wait()
            w_in_ref[i * half:(i + 1) * half, :] = stage_ref[...].astype(_BF16)

        loads[0].start()
        _even_mixer_prep(pool_w_ref.at[0], sgu_w_ref.at[0], sgu_pos_b_ref.at[0],
                         pool_bd_ref, w_tri_ref, sgu_bias_ref, ext_ref, lvl2_ref, lvl4_ref)
        land(0)
        loads[1].start()
        return lambda: land(1)

    def even_step(s, x_ref, small, y_ref, scratch):
        _, pool_scale_ref, sgu_g_ref, sgu_b_ref, _, _ = small
        _even_mixer_kernel(s, x_ref, pool_scale_ref, sgu_g_ref, sgu_b_ref, y_ref, *scratch[:8])

    pool_buf = pltpu.VMEM((bsz, POOL_BASE + SEQ_TILE, POOL_WIDTH), _F32)
    even_scratch = [pltpu.VMEM(ev_w_in.shape[1:], _BF16),
                    pltpu.VMEM((2, 2 * POOL_GROUP_DIM, 2 * POOL_GROUP_DIM), _BF16),
                    pltpu.VMEM((SGU_HEADS, CHUNK, CHUNK), _BF16),
                    pltpu.VMEM((CHUNK, SGU_WIDTH), _F32),
                    pool_buf, pool_buf, pool_buf, pool_buf,
                    pltpu.VMEM((ev_w_in.shape[1] // 2, ev_w_in.shape[2]), _F32),
                    pltpu.SemaphoreType.DMA(())]
    x, (w_in,) = _layer(0, x, p, ln, ev_w_in,
                        (ev_pool_w, ev_pool_scale, ev_sgu_ln_g, ev_sgu_ln_b, ev_sgu_w, ev_sgu_b),
                        even_scratch, even_prep, even_step,
                        channel_w(ev_w_out, 0), [(od_w_in, 0)])

    def odd_prep(w_hbm, small, scratch):
        w_in_ref, _, sem = scratch
        load = pltpu.make_async_copy(w_hbm, w_in_ref, sem)
        load.start()
        return load.wait

    def odd_step(s, x_ref, small, y_ref, scratch):
        w_in_ref, z_ref, _ = scratch
        _odd_mixer_kernel(s, x_ref, w_in_ref, small[0].at[0], y_ref, z_ref)

    odd_scratch = [pltpu.VMEM(w_in.shape, _BF16),
                   pltpu.VMEM((bsz, CONV_HALO + SEQ_TILE, od_conv_w.shape[-1]), _F32),
                   pltpu.SemaphoreType.DMA(())]
    x, _ = _layer(1, x, p, ln, w_in, (od_conv_w,), odd_scratch, odd_prep, odd_step,
                  channel_w(od_w_out, 1), [])
    return x
```

```python
import jax
import jax.numpy as jnp
from jax import lax
from jax.experimental import pallas as pl
from jax.experimental.pallas import tpu as pltpu

POOL_WINDOWS = (2, 4, 8, 16)
POOL_GROUP_DIM = 128
POOL_WIDTH = POOL_GROUP_DIM * len(POOL_WINDOWS)
SGU_HEADS = 4
SGU_HEAD_DIM = 128
SGU_WIDTH = SGU_HEADS * SGU_HEAD_DIM
CHUNK = 128
CONV_WIDTH = 3
LN_EPS = 1e-5
DEPTH = 2
DEEPNORM_ALPHA = (2.0 * DEPTH) ** 0.25

SUBLANES = 8
POOL_HALO = max(POOL_WINDOWS)
POOL_PAD = SUBLANES
POOL_BASE = POOL_PAD + POOL_HALO
CONV_HALO = SUBLANES

SEQ_TILE = 512
SUB_ROWS = 256
LHS_SLOTS = 2
VMEM_LIMIT_BYTES = 60 * 1024 * 1024

_BF16 = jnp.bfloat16
_F32 = jnp.float32


def _dot(a, b):
    return jnp.dot(a, b, preferred_element_type=_F32)


def _layer_norm(x, g, b):
    mu = jnp.mean(x, axis=-1, keepdims=True)
    xc = x - mu
    var = jnp.mean(xc * xc, axis=-1, keepdims=True)
    return xc * lax.rsqrt(var + LN_EPS) * g + b


def _chains(x_ref):
    bsz, ts, _ = x_ref.shape
    return [(b, r) for b in range(bsz) for r in range(0, ts, SUB_ROWS)]


def _layer_row(ref, layer):
    return ref[layer:layer + 1, :]


def _even_mixer_prep(pool_w_ref, sgu_w_ref, sgu_pos_b_ref,
                     pool_bd_ref, w_tri_ref, sgu_bias_ref, ext_ref, lvl2_ref, lvl4_ref):
    g, d, base = POOL_GROUP_DIM, SGU_HEAD_DIM, POOL_BASE
    pool_bd_ref[...] = jnp.zeros(pool_bd_ref.shape, _BF16)
    for k in range(len(POOL_WINDOWS)):
        off = (k % 2) * g
        pool_bd_ref[k // 2, off:off + g, off:off + g] = pool_w_ref[k].astype(_BF16)
    tri = (lax.broadcasted_iota(jnp.int32, (CHUNK, CHUNK), 0)
           >= lax.broadcasted_iota(jnp.int32, (CHUNK, CHUNK), 1))
    for h in range(SGU_HEADS):
        w_tri_ref[h] = jnp.where(tri, sgu_w_ref[h], 0.0).astype(_BF16)
        pos_b = jnp.broadcast_to(sgu_pos_b_ref[h:h + 1, :], (SGU_HEAD_DIM, CHUNK))
        sgu_bias_ref[:, h * d:(h + 1) * d] = pos_b.T
    for b in range(ext_ref.shape[0]):
        ext_ref[b, 0:base, :] = jnp.zeros((base, POOL_WIDTH), _F32)
        lvl2_ref[b, 0:POOL_PAD, :] = jnp.zeros((POOL_PAD, POOL_WIDTH), _F32)
        lvl4_ref[b, 0:POOL_PAD, :] = jnp.zeros((POOL_PAD, POOL_WIDTH), _F32)


def _even_mixer_kernel(s, x_ref, pool_scale_ref, sgu_g_ref, sgu_b_ref, y_ref,
                       w_in_ref, pool_bd_ref, w_tri_ref, sgu_bias_ref,
                       ext_ref, lvl2_ref, lvl4_ref, lvl8_ref):
    bsz, ts, _ = x_ref.shape
    sub, g, d, base = SUB_ROWS, POOL_GROUP_DIM, SGU_HEAD_DIM, POOL_BASE
    chains = _chains(x_ref)

    a, u, v, pooled, vn = {}, {}, {}, {}, {}
    for ch in chains:
        b, r = ch
        xb = x_ref[b, r:r + sub, :].astype(_BF16)
        a[ch] = _dot(xb, w_in_ref[:, 0:POOL_WIDTH])
        ext_ref[b, base + r:base + r + sub, :] = a[ch]
        u[ch] = _dot(xb, w_in_ref[:, POOL_WIDTH:POOL_WIDTH + SGU_WIDTH])
        v[ch] = _dot(xb, w_in_ref[:, POOL_WIDTH + SGU_WIDTH:])

    for ch in chains:
        b, r = ch
        lo = POOL_PAD if r == 0 else base + r
        hi = base + r + sub
        own = slice(base + r, hi)
        lvl2_ref[b, lo:hi, :] = ext_ref[b, lo:hi, :] + ext_ref[b, lo - 1:hi - 1, :]
        lvl4_ref[b, lo:hi, g:] = lvl2_ref[b, lo:hi, g:] + lvl2_ref[b, lo - 2:hi - 2, g:]
        lvl8_ref[b, lo:hi, 2 * g:] = lvl4_ref[b, lo:hi, 2 * g:] + lvl4_ref[b, lo - 4:hi - 4, 2 * g:]
        sums = (
            lvl2_ref[b, own, 0:g],
            lvl4_ref[b, own, g:2 * g],
            lvl8_ref[b, own, 2 * g:3 * g],
            lvl8_ref[b, own, 3 * g:4 * g] + lvl8_ref[b, base + r - 8:hi - 8, 3 * g:4 * g],
        )
        t1 = lax.broadcasted_iota(jnp.int32, (sub, g), 0) + (s * ts + r + 1)
        parts = []
        for gi, w in enumerate(POOL_WINDOWS):
            cnt = jnp.minimum(t1, w).astype(_F32)
            parts.append((sums[gi] / cnt - a[ch][:, gi * g:(gi + 1) * g]).astype(_BF16))
        pooled[ch] = parts
        vn[ch] = _layer_norm(v[ch], sgu_g_ref[...], sgu_b_ref[...]).astype(_BF16)

    for b in range(bsz):
        ext_ref[b, POOL_PAD:base, :] = a[(b, ts - sub)][sub - POOL_HALO:sub, :]

    scale = pool_scale_ref[...]
    n_chunks = sub // CHUNK
    for ch in chains:
        b, r = ch
        for pair in range(2):
            pb = jnp.concatenate(pooled[ch][2 * pair:2 * pair + 2], axis=1)
            ya = _dot(pb, pool_bd_ref[pair]) * scale[:, 2 * g * pair:2 * g * (pair + 1)]
            y_ref[b, r:r + sub, 2 * g * pair:2 * g * (pair + 1)] = ya.astype(_BF16)
        for h in range(SGU_HEADS):
            rhs = jnp.concatenate(
                [vn[ch][c * CHUNK:(c + 1) * CHUNK, h * d:(h + 1) * d] for c in range(n_chunks)],
                axis=1)
            mixed = _dot(w_tri_ref[h], rhs)
            bias_h = sgu_bias_ref[:, h * d:(h + 1) * d]
            for c in range(n_chunks):
                m = mixed[:, c * d:(c + 1) * d] + bias_h
                yb = u[ch][c * CHUNK:(c + 1) * CHUNK, h * d:(h + 1) * d] * m
                y_ref[b, r + c * CHUNK:r + (c + 1) * CHUNK,
                      POOL_WIDTH + h * d:POOL_WIDTH + (h + 1) * d] = yb.astype(_BF16)


def _odd_mixer_kernel(s, x_ref, w_in_ref, conv_w_ref, y_ref, z_ref):
    bsz, ts, _ = x_ref.shape
    sub = SUB_ROWS
    cd = conv_w_ref.shape[-1]
    chains = _chains(x_ref)

    @pl.when(s == 0)
    def _():
        for b in range(bsz):
            z_ref[b, 0:CONV_HALO, :] = jnp.zeros((CONV_HALO, cd), _F32)

    z, gate_b = {}, {}
    for ch in chains:
        b, r = ch
        xb = x_ref[b, r:r + sub, :].astype(_BF16)
        gate_c = _dot(xb, w_in_ref[:, cd:2 * cd])
        hid = _dot(xb, w_in_ref[:, 2 * cd:3 * cd])
        z[ch] = gate_c * hid
        z_ref[b, CONV_HALO + r:CONV_HALO + r + sub, :] = z[ch]
        gate_b[ch] = _dot(xb, w_in_ref[:, 0:cd])
    cw = conv_w_ref[...]
    for ch in chains:
        b, r = ch
        conv = cw[CONV_WIDTH - 1:CONV_WIDTH, :] * z[ch]
        for k in range(CONV_WIDTH - 1):
            off = CONV_HALO + r - (CONV_WIDTH - 1) + k
            conv = conv + cw[k:k + 1, :] * z_ref[b, off:off + sub, :]
        y_ref[b, r:r + sub, :] = (gate_b[ch] * conv).astype(_BF16)
    for b in range(bsz):
        z_ref[b, 0:CONV_HALO, :] = z[(b, ts - sub)][sub - CONV_HALO:sub, :]


def _channel_kernel(x_ref, y_ref, p_ref, w_out_ref, ln1_g_ref, ln1_b_ref, w1_ref, w2_ref,
                    ln2_g_ref, ln2_b_ref, wg_ref, wp_ref, o_ref, xb_ref, hid_ref, x2b_ref,
                    *, layer):
    sub = SUB_ROWS
    gain1, bias1 = _layer_row(ln1_g_ref, layer), _layer_row(ln1_b_ref, layer)
    gain2, bias2 = _layer_row(ln2_g_ref, layer), _layer_row(ln2_b_ref, layer)
    d_ff = w1_ref.shape[1]
    ff_chunk = w1_ref.shape[0]
    chains = _chains(x_ref)

    def out_proj(ch):
        b, r = ch
        return _dot(y_ref[b, r:r + sub, :], w_out_ref[...])

    def mlp(ch, slot, mix):
        b, r = ch
        x1 = _layer_norm(DEEPNORM_ALPHA * x_ref[b, r:r + sub, :] + mix, gain1, bias1)
        xb_ref[slot] = x1.astype(_BF16)
        for c in range(d_ff // ff_chunk):
            h = _dot(xb_ref[slot], w1_ref[:, c * ff_chunk:(c + 1) * ff_chunk])
            h = jnp.maximum(h, 0.0)
            hid_ref[slot, :, c * ff_chunk:(c + 1) * ff_chunk] = (h * h).astype(_BF16)
        acc = _dot(hid_ref[slot], w2_ref[...])
        return _layer_norm(DEEPNORM_ALPHA * x1 + acc, gain2, bias2)

    def embed_gate(ch, slot, x2):
        b, r = ch
        x2b_ref[slot] = x2.astype(_BF16)
        gate = jax.nn.sigmoid(_dot(x2b_ref[slot], wg_ref[...]))
        pp = _dot(p_ref[b, r:r + sub, :].astype(_BF16), wp_ref[...])
        o_ref[b, r:r + sub, :] = x2 + gate * pp

    mix = out_proj(chains[0])
    pending = None
    for i, ch in enumerate(chains):
        next_mix = out_proj(chains[i + 1]) if i + 1 < len(chains) else None
        x2 = mlp(ch, i % LHS_SLOTS, mix)
        if pending is not None:
            embed_gate(*pending)
        pending = (ch, i % LHS_SLOTS, x2)
        mix = next_mix
    embed_gate(*pending)


def _layer(layer, x, p, ln, mixer_w_hbm, mixer_small, mixer_scratch, mixer_prep, mixer_step,
           channel_w, handoff_w):
    bsz, seq, d = x.shape
    steps = seq // SEQ_TILE
    act = pl.BlockSpec((bsz, SEQ_TILE, d), lambda s: (0, s, 0))
    p_in = pl.BlockSpec((bsz, SEQ_TILE, p.shape[-1]), lambda s: (0, s, 0))
    casts = list(channel_w) + list(handoff_w)
    slabs = [w.shape[1] // steps for w, _ in casts]
    assert all(slab * steps == w.shape[1] and slab % (2 * SUBLANES) == 0
               for slab, (w, _) in zip(slabs, casts))
    cast_specs = [pl.BlockSpec((slab, w.shape[2]), lambda s: (s, 0))
                  for slab, (w, _) in zip(slabs, casts)]
    n_small, n_cw, n_hw = len(mixer_small), len(channel_w), len(handoff_w)
    d_ff = channel_w[1][0].shape[-1]
    channel_scratch = [pltpu.VMEM((LHS_SLOTS, SUB_ROWS, d), _BF16),
                       pltpu.VMEM((LHS_SLOTS, SUB_ROWS, d_ff), _BF16),
                       pltpu.VMEM((LHS_SLOTS, SUB_ROWS, d), _BF16)]

    def body(x_hbm, p_hbm, w_hbm, *refs):
        small = refs[:n_small]
        ln_refs = refs[n_small:n_small + 4]
        srcs = refs[n_small + 4:n_small + 4 + n_cw + n_hw]
        k = n_small + 4 + n_cw + n_hw
        o_hbm, y_hbm = refs[k:k + 2]
        handoff = refs[k + 2:k + 2 + n_hw]
        w_refs = refs[k + 2 + n_hw:]

        def mixer_phase(tile_ref, slab_sem, *scratch):
            slab_bufs, scratch = scratch[:n_cw], scratch[n_cw:]
            finish_prep = mixer_prep(w_hbm, small, scratch)
            tile_ref[0] = 0

            def slab_loads(s):
                return [pltpu.make_async_copy(
                            src.at[idx, pl.ds(pl.multiple_of(s * slab, slab), slab), :],
                            buf, slab_sem.at[i])
                        for i, (src, (_, idx), buf, slab)
                        in enumerate(zip(srcs, channel_w, slab_bufs, slabs))]

            def step(x_ref, *tiles):
                src_tiles, y_ref, out_tiles = tiles[:n_hw], tiles[n_hw], tiles[n_hw + 1:]
                s = tile_ref[0]
                if finish_prep is not None:
                    pl.when(s == 0)(finish_prep)
                for load in slab_loads(s):
                    load.start()
                mixer_step(s, x_ref, small, y_ref, scratch)
                for load, buf, dst, slab in zip(slab_loads(s), slab_bufs, w_refs, slabs):
                    load.wait()
                    row = pl.multiple_of(s * slab, slab)
                    dst[pl.ds(row, slab), :] = buf[...].astype(_BF16)
                for src, dst in zip(src_tiles, out_tiles):
                    dst[...] = src[...].astype(_BF16)
                tile_ref[0] = s + 1

            pltpu.emit_pipeline(step, grid=(steps,), in_specs=[act] + cast_specs[n_cw:],
                                out_specs=[act] + cast_specs[n_cw:])(
                x_hbm, *[src.at[idx] for src, (_, idx) in zip(srcs[n_cw:], handoff_w)],
                y_hbm, *handoff)

        pl.run_scoped(mixer_phase, pltpu.SMEM((1,), jnp.int32),
                      pltpu.SemaphoreType.DMA((n_cw,)),
                      *[pltpu.VMEM((slab, w.shape[2]), _F32)
                        for slab, (w, _) in zip(slabs, channel_w)],
                      *mixer_scratch)

        def channel_phase(xb_ref, hid_ref, x2b_ref):
            w_out_ref, w1_ref, w2_ref, wg_ref, wp_ref = w_refs
            g1_ref, b1_ref, g2_ref, b2_ref = ln_refs

            def step(x_ref, y_ref, p_ref, o_ref):
                _channel_kernel(x_ref, y_ref, p_ref, w_out_ref, g1_ref, b1_ref, w1_ref, w2_ref,
                                g2_ref, b2_ref, wg_ref, wp_ref, o_ref, xb_ref, hid_ref, x2b_ref,
                                layer=layer)

            pltpu.emit_pipeline(step, grid=(steps,), in_specs=[act, act, p_in], out_specs=[act])(
                x_hbm, y_hbm, p_hbm.at[layer], o_hbm)

        pl.run_scoped(channel_phase, *channel_scratch)

    hbm = pl.BlockSpec(memory_space=pl.ANY)
    vmem = pl.BlockSpec(memory_space=pltpu.VMEM)
    outs = pl.pallas_call(
        body,
        in_specs=[hbm, hbm, hbm] + [vmem] * (n_small + 4) + [hbm] * (n_cw + n_hw),
        out_specs=[hbm] * (2 + n_hw),
        out_shape=[jax.ShapeDtypeStruct(x.shape, x.dtype), jax.ShapeDtypeStruct(x.shape, _BF16)]
        + [jax.ShapeDtypeStruct(w.shape[1:], _BF16) for w, _ in handoff_w],
        scratch_shapes=[pltpu.VMEM(w.shape[1:], _BF16) for w, _ in channel_w],
        compiler_params=pltpu.CompilerParams(vmem_limit_bytes=VMEM_LIMIT_BYTES),
        name=f"layer_{layer}",
    )(x, p, mixer_w_hbm, *mixer_small, *ln, *[w for w, _ in casts])
    return outs[0], outs[2:]


def kernel(x, p, ev_w_in, ev_pool_w, ev_pool_scale, ev_sgu_ln_g, ev_sgu_ln_b, ev_sgu_w, ev_sgu_b,
           ev_w_out, od_w_in, od_conv_w, od_w_out, ln1_g, ln1_b, ffn_w1, ffn_w2, ln2_g, ln2_b,
           ple_gate_w, ple_w):
    assert x.shape[1] % SEQ_TILE == 0 and SEQ_TILE % SUB_ROWS == 0 and SUB_ROWS % CHUNK == 0
    assert ln1_g.shape[0] == DEPTH and ev_w_in.shape[0] == 1 and od_w_in.shape[0] == 1
    bsz, _, d = x.shape
    ln = (ln1_g, ln1_b, ln2_g, ln2_b)
    channel_w = lambda w_out, i: [(w_out, 0), (ffn_w1, i), (ffn_w2, i), (ple_gate_w, i),
                                  (ple_w, i)]

    def even_prep(w_hbm, small, scratch):
        pool_w_ref, _, _, _, sgu_w_ref, sgu_pos_b_ref = small
        (w_in_ref, pool_bd_ref, w_tri_ref, sgu_bias_ref, ext_ref, lvl2_ref, lvl4_ref, _,
         stage_ref, sem) = scratch
        half = stage_ref.shape[0]
        loads = [pltpu.make_async_copy(w_hbm.at[0, pl.ds(i * half, half), :], stage_ref, sem)
                 for i in range(2)]

        def land(i):
            loads[i].wait()
            w_in_ref[i * half:(i + 1) * half, :] = stage_ref[...].astype(_BF16)

        loads[0].start()
        _even_mixer_prep(pool_w_ref.at[0], sgu_w_ref.at[0], sgu_pos_b_ref.at[0],
                         pool_bd_ref, w_tri_ref, sgu_bias_ref, ext_ref, lvl2_ref, lvl4_ref)
        land(0)
        loads[1].start()
        return lambda: land(1)

    def even_step(s, x_ref, small, y_ref, scratch):
        _, pool_scale_ref, sgu_g_ref, sgu_b_ref, _, _ = small
        _even_mixer_kernel(s, x_ref, pool_scale_ref, sgu_g_ref, sgu_b_ref, y_ref, *scratch[:8])

    pool_buf = pltpu.VMEM((bsz, POOL_BASE + SEQ_TILE, POOL_WIDTH), _F32)
    even_scratch = [pltpu.VMEM(ev_w_in.shape[1:], _BF16),
                    pltpu.VMEM((2, 2 * POOL_GROUP_DIM, 2 * POOL_GROUP_DIM), _BF16),
                    pltpu.VMEM((SGU_HEADS, CHUNK, CHUNK), _BF16),
                    pltpu.VMEM((CHUNK, SGU_WIDTH), _F32),
                    pool_buf, pool_buf, pool_buf, pool_buf,
                    pltpu.VMEM((ev_w_in.shape[1] // 2, ev_w_in.shape[2]), _F32),
                    pltpu.SemaphoreType.DMA(())]
    x, (w_in,) = _layer(0, x, p, ln, ev_w_in,
                        (ev_pool_w, ev_pool_scale, ev_sgu_ln_g, ev_sgu_ln_b, ev_sgu_w, ev_sgu_b),
                        even_scratch, even_prep, even_step,
                        channel_w(ev_w_out, 0), [(od_w_in, 0)])

    def odd_prep(w_hbm, small, scratch):
        w_in_ref, _, sem = scratch
        load = pltpu.make_async_copy(w_hbm, w_in_ref, sem)
        load.start()
        return load.wait

    def odd_step(s, x_ref, small, y_ref, scratch):
        w_in_ref, z_ref, _ = scratch
        _odd_mixer_kernel(s, x_ref, w_in_ref, small[0].at[0], y_ref, z_ref)

    odd_scratch = [pltpu.VMEM(w_in.shape, _BF16),
                   pltpu.VMEM((bsz, CONV_HALO + SEQ_TILE, od_conv_w.shape[-1]), _F32),
                   pltpu.SemaphoreType.DMA(())]
    x, _ = _layer(1, x, p, ln, w_in, (od_conv_w,), odd_scratch, odd_prep, odd_step,
                  channel_w(od_w_out, 1), [])
    return x
```

```python
import jax
import jax.numpy as jnp
from jax import lax
from jax.experimental import pallas as pl
from jax.experimental.pallas import tpu as pltpu

POOL_WINDOWS = (2, 4, 8, 16)
POOL_GROUP_DIM = 128
POOL_WIDTH = POOL_GROUP_DIM * len(POOL_WINDOWS)
SGU_HEADS = 4
SGU_HEAD_DIM = 128
SGU_WIDTH = SGU_HEADS * SGU_HEAD_DIM
CHUNK = 128
CONV_WIDTH = 3
LN_EPS = 1e-5
DEPTH = 2
DEEPNORM_ALPHA = (2.0 * DEPTH) ** 0.25

SUBLANES = 8
POOL_HALO = max(POOL_WINDOWS)
POOL_PAD = SUBLANES
POOL_BASE = POOL_PAD + POOL_HALO
CONV_HALO = SUBLANES

SEQ_TILE = 512
SUB_ROWS = 256
LHS_SLOTS = 2
VMEM_LIMIT_BYTES = 60 * 1024 * 1024

_BF16 = jnp.bfloat16
_F32 = jnp.float32


def _dot(a, b):
    return jnp.dot(a, b, preferred_element_type=_F32)


def _layer_norm(x, g, b):
    mu = jnp.mean(x, axis=-1, keepdims=True)
    xc = x - mu
    var = jnp.mean(xc * xc, axis=-1, keepdims=True)
    return xc * lax.rsqrt(var + LN_EPS) * g + b


def _chains(x_ref):
    bsz, ts, _ = x_ref.shape
    return [(b, r) for b in range(bsz) for r in range(0, ts, SUB_ROWS)]


def _layer_row(ref, layer):
    return ref[layer:layer + 1, :]


def _even_mixer_prep(pool_w_ref, sgu_w_ref, sgu_pos_b_ref,
                     pool_bd_ref, w_tri_ref, sgu_bias_ref, ext_ref, lvl2_ref, lvl4_ref):
    g, d, base = POOL_GROUP_DIM, SGU_HEAD_DIM, POOL_BASE
    pool_bd_ref[...] = jnp.zeros(pool_bd_ref.shape, _BF16)
    for k in range(len(POOL_WINDOWS)):
        off = (k % 2) * g
        pool_bd_ref[k // 2, off:off + g, off:off + g] = pool_w_ref[k].astype(_BF16)
    tri = (lax.broadcasted_iota(jnp.int32, (CHUNK, CHUNK), 0)
           >= lax.broadcasted_iota(jnp.int32, (CHUNK, CHUNK), 1))
    for h in range(SGU_HEADS):
        w_tri_ref[h] = jnp.where(tri, sgu_w_ref[h], 0.0).astype(_BF16)
        pos_b = jnp.broadcast_to(sgu_pos_b_ref[h:h + 1, :], (SGU_HEAD_DIM, CHUNK))
        sgu_bias_ref[:, h * d:(h + 1) * d] = pos_b.T
    for b in range(ext_ref.shape[0]):
        ext_ref[b, 0:base, :] = jnp.zeros((base, POOL_WIDTH), _F32)
        lvl2_ref[b, 0:POOL_PAD, :] = jnp.zeros((POOL_PAD, POOL_WIDTH), _F32)
        lvl4_ref[b, 0:POOL_PAD, :] = jnp.zeros((POOL_PAD, POOL_WIDTH), _F32)


def _even_mixer_kernel(s, x_ref, pool_scale_ref, sgu_g_ref, sgu_b_ref, y_ref,
                       w_in_ref, pool_bd_ref, w_tri_ref, sgu_bias_ref,
                       ext_ref, lvl2_ref, lvl4_ref, lvl8_ref):
    bsz, ts, _ = x_ref.shape
    sub, g, d, base = SUB_ROWS, POOL_GROUP_DIM, SGU_HEAD_DIM, POOL_BASE
    chains = _chains(x_ref)

    a, u, v, pooled, vn = {}, {}, {}, {}, {}
    for ch in chains:
        b, r = ch
        xb = x_ref[b, r:r + sub, :].astype(_BF16)
        a[ch] = _dot(xb, w_in_ref[:, 0:POOL_WIDTH])
        ext_ref[b, base + r:base + r + sub, :] = a[ch]
        u[ch] = _dot(xb, w_in_ref[:, POOL_WIDTH:POOL_WIDTH + SGU_WIDTH])
        v[ch] = _dot(xb, w_in_ref[:, POOL_WIDTH + SGU_WIDTH:])

    for ch in chains:
        b, r = ch
        lo = POOL_PAD if r == 0 else base + r
        hi = base + r + sub
        own = slice(base + r, hi)
        lvl2_ref[b, lo:hi, :] = ext_ref[b, lo:hi, :] + ext_ref[b, lo - 1:hi - 1, :]
        lvl4_ref[b, lo:hi, g:] = lvl2_ref[b, lo:hi, g:] + lvl2_ref[b, lo - 2:hi - 2, g:]
        lvl8_ref[b, lo:hi, 2 * g:] = lvl4_ref[b, lo:hi, 2 * g:] + lvl4_ref[b, lo - 4:hi - 4, 2 * g:]
        sums = (
            lvl2_ref[b, own, 0:g],
            lvl4_ref[b, own, g:2 * g],
            lvl8_ref[b, own, 2 * g:3 * g],
            lvl8_ref[b, own, 3 * g:4 * g] + lvl8_ref[b, base + r - 8:hi - 8, 3 * g:4 * g],
        )
        t1 = lax.broadcasted_iota(jnp.int32, (sub, g), 0) + (s * ts + r + 1)
        parts = []
        for gi, w in enumerate(POOL_WINDOWS):
            cnt = jnp.minimum(t1, w).astype(_F32)
            parts.append((sums[gi] / cnt - a[ch][:, gi * g:(gi + 1) * g]).astype(_BF16))
        pooled[ch] = parts
        vn[ch] = _layer_norm(v[ch], sgu_g_ref[...], sgu_b_ref[...]).astype(_BF16)

    for b in range(bsz):
        ext_ref[b, POOL_PAD:base, :] = a[(b, ts - sub)][sub - POOL_HALO:sub, :]

    scale = pool_scale_ref[...]
    n_chunks = sub // CHUNK
    for ch in chains:
        b, r = ch
        for pair in range(2):
            pb = jnp.concatenate(pooled[ch][2 * pair:2 * pair + 2], axis=1)
            ya = _dot(pb, pool_bd_ref[pair]) * scale[:, 2 * g * pair:2 * g * (pair + 1)]
            y_ref[b, r:r + sub, 2 * g * pair:2 * g * (pair + 1)] = ya.astype(_BF16)
        for h in range(SGU_HEADS):
            rhs = jnp.concatenate(
                [vn[ch][c * CHUNK:(c + 1) * CHUNK, h * d:(h + 1) * d] for c in range(n_chunks)],
                axis=1)
            mixed = _dot(w_tri_ref[h], rhs)
            bias_h = sgu_bias_ref[:, h * d:(h + 1) * d]
            for c in range(n_chunks):
                m = mixed[:, c * d:(c + 1) * d] + bias_h
                yb = u[ch][c * CHUNK:(c + 1) * CHUNK, h * d:(h + 1) * d] * m
                y_ref[b, r + c * CHUNK:r + (c + 1) * CHUNK,
                      POOL_WIDTH + h * d:POOL_WIDTH + (h + 1) * d] = yb.astype(_BF16)


def _odd_mixer_kernel(s, x_ref, w_in_ref, conv_w_ref, y_ref, z_ref):
    bsz, ts, _ = x_ref.shape
    sub = SUB_ROWS
    cd = conv_w_ref.shape[-1]
    chains = _chains(x_ref)

    @pl.when(s == 0)
    def _():
        for b in range(bsz):
            z_ref[b, 0:CONV_HALO, :] = jnp.zeros((CONV_HALO, cd), _F32)

    z, gate_b = {}, {}
    for ch in chains:
        b, r = ch
        xb = x_ref[b, r:r + sub, :].astype(_BF16)
        gate_c = _dot(xb, w_in_ref[:, cd:2 * cd])
        hid = _dot(xb, w_in_ref[:, 2 * cd:3 * cd])
        z[ch] = gate_c * hid
        z_ref[b, CONV_HALO + r:CONV_HALO + r + sub, :] = z[ch]
        gate_b[ch] = _dot(xb, w_in_ref[:, 0:cd])
    cw = conv_w_ref[...]
    for ch in chains:
        b, r = ch
        conv = cw[CONV_WIDTH - 1:CONV_WIDTH, :] * z[ch]
        for k in range(CONV_WIDTH - 1):
            off = CONV_HALO + r - (CONV_WIDTH - 1) + k
            conv = conv + cw[k:k + 1, :] * z_ref[b, off:off + sub, :]
        y_ref[b, r:r + sub, :] = (gate_b[ch] * conv).astype(_BF16)
    for b in range(bsz):
        z_ref[b, 0:CONV_HALO, :] = z[(b, ts - sub)][sub - CONV_HALO:sub, :]


def _channel_kernel(x_ref, y_ref, p_ref, w_out_ref, ln1_g_ref, ln1_b_ref, w1_ref, w2_ref,
                    ln2_g_ref, ln2_b_ref, wg_ref, wp_ref, o_ref, xb_ref, hid_ref, x2b_ref,
                    *, layer):
    sub = SUB_ROWS
    gain1, bias1 = _layer_row(ln1_g_ref, layer), _layer_row(ln1_b_ref, layer)
    gain2, bias2 = _layer_row(ln2_g_ref, layer), _layer_row(ln2_b_ref, layer)
    d_ff = w1_ref.shape[1]
    ff_chunk = w1_ref.shape[0]
    chains = _chains(x_ref)

    def out_proj(ch):
        b, r = ch
        return _dot(y_ref[b, r:r + sub, :], w_out_ref[...])

    def mlp(ch, slot, mix):
        b, r = ch
        x1 = _layer_norm(DEEPNORM_ALPHA * x_ref[b, r:r + sub, :] + mix, gain1, bias1)
        xb_ref[slot] = x1.astype(_BF16)
        for c in range(d_ff // ff_chunk):
            h = _dot(xb_ref[slot], w1_ref[:, c * ff_chunk:(c + 1) * ff_chunk])
            h = jnp.maximum(h, 0.0)
            hid_ref[slot, :, c * ff_chunk:(c + 1) * ff_chunk] = (h * h).astype(_BF16)
        acc = _dot(hid_ref[slot], w2_ref[...])
        return _layer_norm(DEEPNORM_ALPHA * x1 + acc, gain2, bias2)

    def embed_gate(ch, slot, x2):
        b, r = ch
        x2b_ref[slot] = x2.astype(_BF16)
        gate = jax.nn.sigmoid(_dot(x2b_ref[slot], wg_ref[...]))
        pp = _dot(p_ref[b, r:r + sub, :].astype(_BF16), wp_ref[...])
        o_ref[b, r:r + sub, :] = x2 + gate * pp

    mix = out_proj(chains[0])
    pending = None
    for i, ch in enumerate(chains):
        next_mix = out_proj(chains[i + 1]) if i + 1 < len(chains) else None
        x2 = mlp(ch, i % LHS_SLOTS, mix)
        if pending is not None:
            embed_gate(*pending)
        pending = (ch, i % LHS_SLOTS, x2)
        mix = next_mix
    embed_gate(*pending)


def _layer(layer, x, p, ln, mixer_w_hbm, mixer_small, mixer_scratch, mixer_prep, mixer_step,
           channel_w, handoff_w):
    bsz, seq, d = x.shape
    steps = seq // SEQ_TILE
    act = pl.BlockSpec((bsz, SEQ_TILE, d), lambda s: (0, s, 0))
    p_in = pl.BlockSpec((bsz, SEQ_TILE, p.shape[-1]), lambda s: (0, s, 0))
    casts = list(channel_w) + list(handoff_w)
    slabs = [w.shape[1] // steps for w, _ in casts]
    assert all(slab * steps == w.shape[1] and slab % (2 * SUBLANES) == 0
               for slab, (w, _) in zip(slabs, casts))
    cast_specs = [pl.BlockSpec((slab, w.shape[2]), lambda s: (s, 0))
                  for slab, (w, _) in zip(slabs, casts)]
    n_small, n_cw, n_hw = len(mixer_small), len(channel_w), len(handoff_w)
    d_ff = channel_w[1][0].shape[-1]
    channel_scratch = [pltpu.VMEM((LHS_SLOTS, SUB_ROWS, d), _BF16),
                       pltpu.VMEM((LHS_SLOTS, SUB_ROWS, d_ff), _BF16),
                       pltpu.VMEM((LHS_SLOTS, SUB_ROWS, d), _BF16)]

    def body(x_hbm, p_hbm, w_hbm, *refs):
        small = refs[:n_small]
        ln_refs = refs[n_small:n_small + 4]
        srcs = refs[n_small + 4:n_small + 4 + n_cw + n_hw]
        k = n_small + 4 + n_cw + n_hw
        o_hbm, y_hbm = refs[k:k + 2]
        handoff = refs[k + 2:k + 2 + n_hw]
        w_refs = refs[k + 2 + n_hw:]

        def mixer_phase(tile_ref, slab_sem, *scratch):
            slab_bufs, scratch = scratch[:n_cw], scratch[n_cw:]
            finish_prep = mixer_prep(w_hbm, small, scratch)
            tile_ref[0] = 0

            def slab_loads(s):
                return [pltpu.make_async_copy(
                            src.at[idx, pl.ds(pl.multiple_of(s * slab, slab), slab), :],
                            buf, slab_sem.at[i])
                        for i, (src, (_, idx), buf, slab)
                        in enumerate(zip(srcs, channel_w, slab_bufs, slabs))]

            def step(x_ref, *tiles):
                src_tiles, y_ref, out_tiles = tiles[:n_hw], tiles[n_hw], tiles[n_hw + 1:]
                s = tile_ref[0]
                for load in slab_loads(s):
                    load.start(priority=1)
                if finish_prep is not None:
                    pl.when(s == 0)(finish_prep)
                mixer_step(s, x_ref, small, y_ref, scratch)
                for load, buf, dst, slab in zip(slab_loads(s), slab_bufs, w_refs, slabs):
                    load.wait()
                    row = pl.multiple_of(s * slab, slab)
                    dst[pl.ds(row, slab), :] = buf[...].astype(_BF16)
                for src, dst in zip(src_tiles, out_tiles):
                    dst[...] = src[...].astype(_BF16)
                tile_ref[0] = s + 1

            pltpu.emit_pipeline(step, grid=(steps,), in_specs=[act] + cast_specs[n_cw:],
                                out_specs=[act] + cast_specs[n_cw:])(
                x_hbm, *[src.at[idx] for src, (_, idx) in zip(srcs[n_cw:], handoff_w)],
                y_hbm, *handoff)

        pl.run_scoped(mixer_phase, pltpu.SMEM((1,), jnp.int32),
                      pltpu.SemaphoreType.DMA((n_cw,)),
                      *[pltpu.VMEM((slab, w.shape[2]), _F32)
                        for slab, (w, _) in zip(slabs, channel_w)],
                      *mixer_scratch)

        def channel_phase(xb_ref, hid_ref, x2b_ref):
            w_out_ref, w1_ref, w2_ref, wg_ref, wp_ref = w_refs
            g1_ref, b1_ref, g2_ref, b2_ref = ln_refs

            def step(x_ref, y_ref, p_ref, o_ref):
                _channel_kernel(x_ref, y_ref, p_ref, w_out_ref, g1_ref, b1_ref, w1_ref, w2_ref,
                                g2_ref, b2_ref, wg_ref, wp_ref, o_ref, xb_ref, hid_ref, x2b_ref,
                                layer=layer)

            pltpu.emit_pipeline(step, grid=(steps,), in_specs=[act, act, p_in], out_specs=[act])(
                x_hbm, y_hbm, p_hbm.at[layer], o_hbm)

        pl.run_scoped(channel_phase, *channel_scratch)

    hbm = pl.BlockSpec(memory_space=pl.ANY)
    vmem = pl.BlockSpec(memory_space=pltpu.VMEM)
    outs = pl.pallas_call(
        body,
        in_specs=[hbm, hbm, hbm] + [vmem] * (n_small + 4) + [hbm] * (n_cw + n_hw),
        out_specs=[hbm] * (2 + n_hw),
        out_shape=[jax.ShapeDtypeStruct(x.shape, x.dtype), jax.ShapeDtypeStruct(x.shape, _BF16)]
        + [jax.ShapeDtypeStruct(w.shape[1:], _BF16) for w, _ in handoff_w],
        scratch_shapes=[pltpu.VMEM(w.shape[1:], _BF16) for w, _ in channel_w],
        compiler_params=pltpu.CompilerParams(vmem_limit_bytes=VMEM_LIMIT_BYTES),
        name=f"layer_{layer}",
    )(x, p, mixer_w_hbm, *mixer_small, *ln, *[w for w, _ in casts])
    return outs[0], outs[2:]


def kernel(x, p, ev_w_in, ev_pool_w, ev_pool_scale, ev_sgu_ln_g, ev_sgu_ln_b, ev_sgu_w, ev_sgu_b,
           ev_w_out, od_w_in, od_conv_w, od_w_out, ln1_g, ln1_b, ffn_w1, ffn_w2, ln2_g, ln2_b,
           ple_gate_w, ple_w):
    assert x.shape[1] % SEQ_TILE == 0 and SEQ_TILE % SUB_ROWS == 0 and SUB_ROWS % CHUNK == 0
    assert ln1_g.shape[0] == DEPTH and ev_w_in.shape[0] == 1 and od_w_in.shape[0] == 1
    bsz, _, d = x.shape
    ln = (ln1_g, ln1_b, ln2_g, ln2_b)
    channel_w = lambda w_out, i: [(w_out, 0), (ffn_w1, i), (ffn_w2, i), (ple_gate_w, i),
                                  (ple_w, i)]

    def even_prep(w_hbm, small, scratch):
        pool_w_ref, _, _, _, sgu_w_ref, sgu_pos_b_ref = small
        (w_in_ref, pool_bd_ref, w_tri_ref, sgu_bias_ref, ext_ref, lvl2_ref, lvl4_ref, _,
         stage_ref, sem) = scratch
        half = stage_ref.shape[0]
        loads = [pltpu.make_async_copy(w_hbm.at[0, pl.ds(i * half, half), :], stage_ref, sem)
                 for i in range(2)]

        def land(i):
            loads[i].wait()
            w_in_ref[i * half:(i + 1) * half, :] = stage_ref[...].astype(_BF16)

        loads[0].start()
        _even_mixer_prep(pool_w_ref.at[0], sgu_w_ref.at[0], sgu_pos_b_ref.at[0],
                         pool_bd_ref, w_tri_ref, sgu_bias_ref, ext_ref, lvl2_ref, lvl4_ref)
        land(0)
        loads[1].start()
        return lambda: land(1)

    def even_step(s, x_ref, small, y_ref, scratch):
        _, pool_scale_ref, sgu_g_ref, sgu_b_ref, _, _ = small
        _even_mixer_kernel(s, x_ref, pool_scale_ref, sgu_g_ref, sgu_b_ref, y_ref, *scratch[:8])

    pool_buf = pltpu.VMEM((bsz, POOL_BASE + SEQ_TILE, POOL_WIDTH), _F32)
    even_scratch = [pltpu.VMEM(ev_w_in.shape[1:], _BF16),
                    pltpu.VMEM((2, 2 * POOL_GROUP_DIM, 2 * POOL_GROUP_DIM), _BF16),
                    pltpu.VMEM((SGU_HEADS, CHUNK, CHUNK), _BF16),
                    pltpu.VMEM((CHUNK, SGU_WIDTH), _F32),
                    pool_buf, pool_buf, pool_buf, pool_buf,
                    pltpu.VMEM((ev_w_in.shape[1] // 2, ev_w_in.shape[2]), _F32),
                    pltpu.SemaphoreType.DMA(())]
    x, (w_in,) = _layer(0, x, p, ln, ev_w_in,
                        (ev_pool_w, ev_pool_scale, ev_sgu_ln_g, ev_sgu_ln_b, ev_sgu_w, ev_sgu_b),
                        even_scratch, even_prep, even_step,
                        channel_w(ev_w_out, 0), [(od_w_in, 0)])

    def odd_prep(w_hbm, small, scratch):
        w_in_ref, _, sem = scratch
        load = pltpu.make_async_copy(w_hbm, w_in_ref, sem)
        load.start()
        return load.wait

    def odd_step(s, x_ref, small, y_ref, scratch):
        w_in_ref, z_ref, _ = scratch
        _odd_mixer_kernel(s, x_ref, w_in_ref, small[0].at[0], y_ref, z_ref)

    odd_scratch = [pltpu.VMEM(w_in.shape, _BF16),
                   pltpu.VMEM((bsz, CONV_HALO + SEQ_TILE, od_conv_w.shape[-1]), _F32),
                   pltpu.SemaphoreType.DMA(())]
    x, _ = _layer(1, x, p, ln, w_in, (od_conv_w,), odd_scratch, odd_prep, odd_step,
                  channel_w(od_w_out, 1), [])
    return x
```

```python
import jax
import jax.numpy as jnp
from jax import lax
from jax.experimental import pallas as pl
from jax.experimental.pallas import tpu as pltpu

POOL_WINDOWS = (2, 4, 8, 16)
POOL_GROUP_DIM = 128
POOL_WIDTH = POOL_GROUP_DIM * len(POOL_WINDOWS)
SGU_HEADS = 4
SGU_HEAD_DIM = 128
SGU_WIDTH = SGU_HEADS * SGU_HEAD_DIM
CHUNK = 128
CONV_WIDTH = 3
LN_EPS = 1e-5
DEPTH = 2
DEEPNORM_ALPHA = (2.0 * DEPTH) ** 0.25

SUBLANES = 8
POOL_HALO = max(POOL_WINDOWS)
POOL_PAD = SUBLANES
POOL_BASE = POOL_PAD + POOL_HALO
CONV_HALO = SUBLANES

SEQ_TILE = 512
SUB_ROWS = 256
LHS_SLOTS = 2
VMEM_LIMIT_BYTES = 60 * 1024 * 1024

_BF16 = jnp.bfloat16
_F32 = jnp.float32


def _dot(a, b):
    return jnp.dot(a, b, preferred_element_type=_F32)


def _layer_norm(x, g, b):
    mu = jnp.mean(x, axis=-1, keepdims=True)
    xc = x - mu
    var = jnp.mean(xc * xc, axis=-1, keepdims=True)
    return xc * lax.rsqrt(var + LN_EPS) * g + b


def _chains(x_ref):
    bsz, ts, _ = x_ref.shape
    return [(b, r) for b in range(bsz) for r in range(0, ts, SUB_ROWS)]


def _layer_row(ref, layer):
    return ref[layer:layer + 1, :]


def _even_mixer_prep(pool_w_ref, sgu_w_ref, sgu_pos_b_ref,
                     pool_bd_ref, w_tri_ref, sgu_bias_ref, ext_ref, lvl2_ref, lvl4_ref):
    g, d, base = POOL_GROUP_DIM, SGU_HEAD_DIM, POOL_BASE
    pool_bd_ref[...] = jnp.zeros(pool_bd_ref.shape, _BF16)
    for k in range(len(POOL_WINDOWS)):
        off = (k % 2) * g
        pool_bd_ref[k // 2, off:off + g, off:off + g] = pool_w_ref[k].astype(_BF16)
    tri = (lax.broadcasted_iota(jnp.int32, (CHUNK, CHUNK), 0)
           >= lax.broadcasted_iota(jnp.int32, (CHUNK, CHUNK), 1))
    for h in range(SGU_HEADS):
        w_tri_ref[h] = jnp.where(tri, sgu_w_ref[h], 0.0).astype(_BF16)
        pos_b = jnp.broadcast_to(sgu_pos_b_ref[h:h + 1, :], (SGU_HEAD_DIM, CHUNK))
        sgu_bias_ref[:, h * d:(h + 1) * d] = pos_b.T
    for b in range(ext_ref.shape[0]):
        ext_ref[b, 0:base, :] = jnp.zeros((base, POOL_WIDTH), _F32)
        lvl2_ref[b, 0:POOL_PAD, :] = jnp.zeros((POOL_PAD, POOL_WIDTH), _F32)
        lvl4_ref[b, 0:POOL_PAD, :] = jnp.zeros((POOL_PAD, POOL_WIDTH), _F32)


def _even_mixer_kernel(s, x_ref, pool_scale_ref, sgu_g_ref, sgu_b_ref, y_ref,
                       w_in_ref, pool_bd_ref, w_tri_ref, sgu_bias_ref,
                       ext_ref, lvl2_ref, lvl4_ref, lvl8_ref):
    bsz, ts, _ = x_ref.shape
    sub, g, d, base = SUB_ROWS, POOL_GROUP_DIM, SGU_HEAD_DIM, POOL_BASE
    chains = _chains(x_ref)

    a, u, v, pooled, vn = {}, {}, {}, {}, {}
    for ch in chains:
        b, r = ch
        xb = x_ref[b, r:r + sub, :].astype(_BF16)
        a[ch] = _dot(xb, w_in_ref[:, 0:POOL_WIDTH])
        ext_ref[b, base + r:base + r + sub, :] = a[ch]
        u[ch] = _dot(xb, w_in_ref[:, POOL_WIDTH:POOL_WIDTH + SGU_WIDTH])
        v[ch] = _dot(xb, w_in_ref[:, POOL_WIDTH + SGU_WIDTH:])

    for ch in chains:
        b, r = ch
        lo = POOL_PAD if r == 0 else base + r
        hi = base + r + sub
        own = slice(base + r, hi)
        lvl2_ref[b, lo:hi, :] = ext_ref[b, lo:hi, :] + ext_ref[b, lo - 1:hi - 1, :]
        lvl4_ref[b, lo:hi, g:] = lvl2_ref[b, lo:hi, g:] + lvl2_ref[b, lo - 2:hi - 2, g:]
        lvl8_ref[b, lo:hi, 2 * g:] = lvl4_ref[b, lo:hi, 2 * g:] + lvl4_ref[b, lo - 4:hi - 4, 2 * g:]
        sums = (
            lvl2_ref[b, own, 0:g],
            lvl4_ref[b, own, g:2 * g],
            lvl8_ref[b, own, 2 * g:3 * g],
            lvl8_ref[b, own, 3 * g:4 * g] + lvl8_ref[b, base + r - 8:hi - 8, 3 * g:4 * g],
        )
        t1 = lax.broadcasted_iota(jnp.int32, (sub, g), 0) + (s * ts + r + 1)
        parts = []
        for gi, w in enumerate(POOL_WINDOWS):
            cnt = jnp.minimum(t1, w).astype(_F32)
            parts.append((sums[gi] / cnt - a[ch][:, gi * g:(gi + 1) * g]).astype(_BF16))
        pooled[ch] = parts
        vn[ch] = _layer_norm(v[ch], sgu_g_ref[...], sgu_b_ref[...]).astype(_BF16)

    for b in range(bsz):
        ext_ref[b, POOL_PAD:base, :] = a[(b, ts - sub)][sub - POOL_HALO:sub, :]

    scale = pool_scale_ref[...]
    n_chunks = sub // CHUNK
    for ch in chains:
        b, r = ch
        for pair in range(2):
            pb = jnp.concatenate(pooled[ch][2 * pair:2 * pair + 2], axis=1)
            ya = _dot(pb, pool_bd_ref[pair]) * scale[:, 2 * g * pair:2 * g * (pair + 1)]
            y_ref[b, r:r + sub, 2 * g * pair:2 * g * (pair + 1)] = ya.astype(_BF16)
        for h in range(SGU_HEADS):
            rhs = jnp.concatenate(
                [vn[ch][c * CHUNK:(c + 1) * CHUNK, h * d:(h + 1) * d] for c in range(n_chunks)],
                axis=1)
            mixed = _dot(w_tri_ref[h], rhs)
            bias_h = sgu_bias_ref[:, h * d:(h + 1) * d]
            for c in range(n_chunks):
                m = mixed[:, c * d:(c + 1) * d] + bias_h
                yb = u[ch][c * CHUNK:(c + 1) * CHUNK, h * d:(h + 1) * d] * m
                y_ref[b, r + c * CHUNK:r + (c + 1) * CHUNK,
                      POOL_WIDTH + h * d:POOL_WIDTH + (h + 1) * d] = yb.astype(_BF16)


def _odd_mixer_kernel(s, x_ref, w_in_ref, conv_w_ref, y_ref, z_ref):
    bsz, ts, _ = x_ref.shape
    sub = SUB_ROWS
    cd = conv_w_ref.shape[-1]
    chains = _chains(x_ref)

    @pl.when(s == 0)
    def _():
        for b in range(bsz):
            z_ref[b, 0:CONV_HALO, :] = jnp.zeros((CONV_HALO, cd), _F32)

    z, gate_b = {}, {}
    for ch in chains:
        b, r = ch
        xb = x_ref[b, r:r + sub, :].astype(_BF16)
        gate_c = _dot(xb, w_in_ref[:, cd:2 * cd])
        hid = _dot(xb, w_in_ref[:, 2 * cd:3 * cd])
        z[ch] = gate_c * hid
        z_ref[b, CONV_HALO + r:CONV_HALO + r + sub, :] = z[ch]
        gate_b[ch] = _dot(xb, w_in_ref[:, 0:cd])
    cw = conv_w_ref[...]
    for ch in chains:
        b, r = ch
        conv = cw[CONV_WIDTH - 1:CONV_WIDTH, :] * z[ch]
        for k in range(CONV_WIDTH - 1):
            off = CONV_HALO + r - (CONV_WIDTH - 1) + k
            conv = conv + cw[k:k + 1, :] * z_ref[b, off:off + sub, :]
        y_ref[b, r:r + sub, :] = (gate_b[ch] * conv).astype(_BF16)
    for b in range(bsz):
        z_ref[b, 0:CONV_HALO, :] = z[(b, ts - sub)][sub - CONV_HALO:sub, :]


def _channel_kernel(x_ref, y_ref, p_ref, w_out_ref, ln1_g_ref, ln1_b_ref, w1_ref, w2_ref,
                    ln2_g_ref, ln2_b_ref, wg_ref, wp_ref, o_ref, xb_ref, hid_ref, x2b_ref,
                    *, layer):
    sub = SUB_ROWS
    gain1, bias1 = _layer_row(ln1_g_ref, layer), _layer_row(ln1_b_ref, layer)
    gain2, bias2 = _layer_row(ln2_g_ref, layer), _layer_row(ln2_b_ref, layer)
    d_ff = w1_ref.shape[1]
    ff_chunk = w1_ref.shape[0]
    chains = _chains(x_ref)

    def out_proj(ch):
        b, r = ch
        return _dot(y_ref[b, r:r + sub, :], w_out_ref[...])

    def mlp(ch, slot, mix):
        b, r = ch
        x1 = _layer_norm(DEEPNORM_ALPHA * x_ref[b, r:r + sub, :] + mix, gain1, bias1)
        xb_ref[slot] = x1.astype(_BF16)
        for c in range(d_ff // ff_chunk):
            h = _dot(xb_ref[slot], w1_ref[:, c * ff_chunk:(c + 1) * ff_chunk])
            h = jnp.maximum(h, 0.0)
            hid_ref[slot, :, c * ff_chunk:(c + 1) * ff_chunk] = (h * h).astype(_BF16)
        acc = _dot(hid_ref[slot], w2_ref[...])
        return _layer_norm(DEEPNORM_ALPHA * x1 + acc, gain2, bias2)

    def embed_gate(ch, slot, x2):
        b, r = ch
        x2b_ref[slot] = x2.astype(_BF16)
        gate = jax.nn.sigmoid(_dot(x2b_ref[slot], wg_ref[...]))
        pp = _dot(p_ref[b, r:r + sub, :].astype(_BF16), wp_ref[...])
        o_ref[b, r:r + sub, :] = x2 + gate * pp

    mix = out_proj(chains[0])
    pending = None
    for i, ch in enumerate(chains):
        next_mix = out_proj(chains[i + 1]) if i + 1 < len(chains) else None
        x2 = mlp(ch, i % LHS_SLOTS, mix)
        if pending is not None:
            embed_gate(*pending)
        pending = (ch, i % LHS_SLOTS, x2)
        mix = next_mix
    embed_gate(*pending)


def _layer(layer, x, p, ln, mixer_w_hbm, mixer_small, mixer_scratch, mixer_prep, mixer_step,
           channel_w, handoff_w):
    bsz, seq, d = x.shape
    steps = seq // SEQ_TILE
    act = pl.BlockSpec((bsz, SEQ_TILE, d), lambda s: (0, s, 0))
    p_in = pl.BlockSpec((bsz, SEQ_TILE, p.shape[-1]), lambda s: (0, s, 0))
    casts = list(channel_w) + list(handoff_w)
    slabs = [w.shape[1] // steps for w, _ in casts]
    assert all(slab * steps == w.shape[1] and slab % (2 * SUBLANES) == 0
               for slab, (w, _) in zip(slabs, casts))
    cast_specs = [pl.BlockSpec((slab, w.shape[2]), lambda s: (s, 0))
                  for slab, (w, _) in zip(slabs, casts)]
    n_small, n_cw, n_hw = len(mixer_small), len(channel_w), len(handoff_w)
    d_ff = channel_w[1][0].shape[-1]
    channel_scratch = [pltpu.VMEM((LHS_SLOTS, SUB_ROWS, d), _BF16),
                       pltpu.VMEM((LHS_SLOTS, SUB_ROWS, d_ff), _BF16),
                       pltpu.VMEM((LHS_SLOTS, SUB_ROWS, d), _BF16)]

    def body(x_hbm, p_hbm, w_hbm, *refs):
        small = refs[:n_small]
        ln_refs = refs[n_small:n_small + 4]
        srcs = refs[n_small + 4:n_small + 4 + n_cw + n_hw]
        k = n_small + 4 + n_cw + n_hw
        o_hbm, y_hbm = refs[k:k + 2]
        handoff = refs[k + 2:k + 2 + n_hw]
        w_refs = refs[k + 2 + n_hw:]

        def mixer_phase(tile_ref, slab_sem, *scratch):
            slab_bufs, scratch = scratch[:n_cw], scratch[n_cw:]
            finish_prep = mixer_prep(w_hbm, small, scratch)
            tile_ref[0] = 0

            def slab_loads(s):
                return [pltpu.make_async_copy(
                            src.at[idx, pl.ds(pl.multiple_of(s * slab, slab), slab), :],
                            buf, slab_sem.at[i])
                        for i, (src, (_, idx), buf, slab)
                        in enumerate(zip(srcs, channel_w, slab_bufs, slabs))]

            def step(x_ref, *tiles):
                src_tiles, y_ref, out_tiles = tiles[:n_hw], tiles[n_hw], tiles[n_hw + 1:]
                s = tile_ref[0]
                for load in slab_loads(s):
                    load.start(priority=1)
                if finish_prep is not None:
                    pl.when(s == 0)(finish_prep)
                mixer_step(s, x_ref, small, y_ref, scratch)
                for load, buf, dst, slab in zip(slab_loads(s), slab_bufs, w_refs, slabs):
                    load.wait()
                    row = pl.multiple_of(s * slab, slab)
                    dst[pl.ds(row, slab), :] = buf[...].astype(_BF16)
                for src, dst in zip(src_tiles, out_tiles):
                    dst[...] = src[...].astype(_BF16)
                tile_ref[0] = s + 1

            pltpu.emit_pipeline(step, grid=(steps,), in_specs=[act] + cast_specs[n_cw:],
                                out_specs=[act] + cast_specs[n_cw:])(
                x_hbm, *[src.at[idx] for src, (_, idx) in zip(srcs[n_cw:], handoff_w)],
                y_hbm, *handoff)

        pl.run_scoped(mixer_phase, pltpu.SMEM((1,), jnp.int32),
                      pltpu.SemaphoreType.DMA((n_cw,)),
                      *[pltpu.VMEM((slab, w.shape[2]), _F32)
                        for slab, (w, _) in zip(slabs, channel_w)],
                      *mixer_scratch)

        def channel_phase(xb_ref, hid_ref, x2b_ref):
            w_out_ref, w1_ref, w2_ref, wg_ref, wp_ref = w_refs
            g1_ref, b1_ref, g2_ref, b2_ref = ln_refs

            def step(x_ref, y_ref, p_ref, o_ref):
                _channel_kernel(x_ref, y_ref, p_ref, w_out_ref, g1_ref, b1_ref, w1_ref, w2_ref,
                                g2_ref, b2_ref, wg_ref, wp_ref, o_ref, xb_ref, hid_ref, x2b_ref,
                                layer=layer)

            pltpu.emit_pipeline(step, grid=(steps,), in_specs=[act, act, p_in], out_specs=[act])(
                x_hbm, y_hbm, p_hbm.at[layer], o_hbm)

        pl.run_scoped(channel_phase, *channel_scratch)

    hbm = pl.BlockSpec(memory_space=pl.ANY)
    vmem = pl.BlockSpec(memory_space=pltpu.VMEM)
    outs = pl.pallas_call(
        body,
        in_specs=[hbm, hbm, hbm] + [vmem] * (n_small + 4) + [hbm] * (n_cw + n_hw),
        out_specs=[hbm] * (2 + n_hw),
        out_shape=[jax.ShapeDtypeStruct(x.shape, x.dtype), jax.ShapeDtypeStruct(x.shape, _BF16)]
        + [jax.ShapeDtypeStruct(w.shape[1:], _BF16) for w, _ in handoff_w],
        scratch_shapes=[pltpu.VMEM(w.shape[1:], _BF16) for w, _ in channel_w],
        compiler_params=pltpu.CompilerParams(vmem_limit_bytes=VMEM_LIMIT_BYTES),
        name=f"layer_{layer}",
    )(x, p, mixer_w_hbm, *mixer_small, *ln, *[w for w, _ in casts])
    return outs[0], outs[2:]


def kernel(x, p, ev_w_in, ev_pool_w, ev_pool_scale, ev_sgu_ln_g, ev_sgu_ln_b, ev_sgu_w, ev_sgu_b,
           ev_w_out, od_w_in, od_conv_w, od_w_out, ln1_g, ln1_b, ffn_w1, ffn_w2, ln2_g, ln2_b,
           ple_gate_w, ple_w):
    assert x.shape[1] % SEQ_TILE == 0 and SEQ_TILE % SUB_ROWS == 0 and SUB_ROWS % CHUNK == 0
    assert ln1_g.shape[0] == DEPTH and ev_w_in.shape[0] == 1 and od_w_in.shape[0] == 1
    bsz, _, d = x.shape
    ln = (ln1_g, ln1_b, ln2_g, ln2_b)
    channel_w = lambda w_out, i: [(w_out, 0), (ffn_w1, i), (ffn_w2, i), (ple_gate_w, i),
                                  (ple_w, i)]

    def even_prep(w_hbm, small, scratch):
        pool_w_ref, _, _, _, sgu_w_ref, sgu_pos_b_ref = small
        (w_in_ref, pool_bd_ref, w_tri_ref, sgu_bias_ref, ext_ref, lvl2_ref, lvl4_ref, _,
         stage_ref, sem) = scratch
        half = stage_ref.shape[0]
        loads = [pltpu.make_async_copy(w_hbm.at[0, pl.ds(i * half, half), :], stage_ref, sem)
                 for i in range(2)]

        def land(i):
            loads[i].wait()
            w_in_ref[i * half:(i + 1) * half, :] = stage_ref[...].astype(_BF16)

        loads[0].start()
        _even_mixer_prep(pool_w_ref.at[0], sgu_w_ref.at[0], sgu_pos_b_ref.at[0],
                         pool_bd_ref, w_tri_ref, sgu_bias_ref, ext_ref, lvl2_ref, lvl4_ref)
        land(0)
        loads[1].start(priority=1)
        return lambda: land(1)

    def even_step(s, x_ref, small, y_ref, scratch):
        _, pool_scale_ref, sgu_g_ref, sgu_b_ref, _, _ = small
        _even_mixer_kernel(s, x_ref, pool_scale_ref, sgu_g_ref, sgu_b_ref, y_ref, *scratch[:8])

    pool_buf = pltpu.VMEM((bsz, POOL_BASE + SEQ_TILE, POOL_WIDTH), _F32)
    even_scratch = [pltpu.VMEM(ev_w_in.shape[1:], _BF16),
                    pltpu.VMEM((2, 2 * POOL_GROUP_DIM, 2 * POOL_GROUP_DIM), _BF16),
                    pltpu.VMEM((SGU_HEADS, CHUNK, CHUNK), _BF16),
                    pltpu.VMEM((CHUNK, SGU_WIDTH), _F32),
                    pool_buf, pool_buf, pool_buf, pool_buf,
                    pltpu.VMEM((ev_w_in.shape[1] // 2, ev_w_in.shape[2]), _F32),
                    pltpu.SemaphoreType.DMA(())]
    x, (w_in,) = _layer(0, x, p, ln, ev_w_in,
                        (ev_pool_w, ev_pool_scale, ev_sgu_ln_g, ev_sgu_ln_b, ev_sgu_w, ev_sgu_b),
                        even_scratch, even_prep, even_step,
                        channel_w(ev_w_out, 0), [(od_w_in, 0)])

    def odd_prep(w_hbm, small, scratch):
        w_in_ref, _, sem = scratch
        load = pltpu.make_async_copy(w_hbm, w_in_ref, sem)
        load.start(priority=1)
        return load.wait

    def odd_step(s, x_ref, small, y_ref, scratch):
        w_in_ref, z_ref, _ = scratch
        _odd_mixer_kernel(s, x_ref, w_in_ref, small[0].at[0], y_ref, z_ref)

    odd_scratch = [pltpu.VMEM(w_in.shape, _BF16),
                   pltpu.VMEM((bsz, CONV_HALO + SEQ_TILE, od_conv_w.shape[-1]), _F32),
                   pltpu.SemaphoreType.DMA(())]
    x, _ = _layer(1, x, p, ln, w_in, (od_conv_w,), odd_scratch, odd_prep, odd_step,
                  channel_w(od_w_out, 1), [])
    return x
```
